```python
import math
import jax, jax.numpy as jnp
from jax import lax
import numpy as np

D_MODEL = 4096
BATCH = 8
SEQ = 2048
DEPTH = 2
DEC_BATCH = 2
DEC_SEQ = 8192
PAST_LEN = 128

HA_HEADS = 16
HA_HEAD = 128
HA_WIDTH = HA_HEADS * HA_HEAD
DB_HEADS = 8
DB_HEAD = 128
DB_QK_WIDTH = DB_HEADS * 2 * DB_HEAD
DB_V_WIDTH = DB_HEADS * 2 * DB_HEAD
MIX_WIDTH = HA_WIDTH + DB_V_WIDTH
IN_COLS = 5 * HA_WIDTH + 2 * DB_QK_WIDTH + DB_V_WIDTH
D_FF = 11008
CONV_WIDTH = 3
CHUNK = 64
Q_BLOCK = 128
NUM_BUCKETS = 32
MAX_DISTANCE = 128
EPS = 1e-6

kernel_name = "hymba_hgrn2_diffattn_convffn_adaln_encoder"


def rmsnorm(x, w):
    xf = x.astype(jnp.float32)
    y = xf * lax.rsqrt(jnp.mean(xf * xf, axis=-1, keepdims=True) + EPS)
    return (y * w.astype(jnp.float32)).astype(x.dtype)


def rel_bucket(rel):
    nb = NUM_BUCKETS // 2
    max_exact = nb // 2
    ret = (rel > 0).astype(jnp.int32) * nb
    n = jnp.abs(rel)
    nf = jnp.maximum(n, 1).astype(jnp.float32)
    large = max_exact + (jnp.log(nf / max_exact) / math.log(MAX_DISTANCE / max_exact)
                         * (nb - max_exact)).astype(jnp.int32)
    large = jnp.minimum(large, nb - 1)
    return ret + jnp.where(n < max_exact, n, large)


def gla_chunk(q, k, v, logf):
    B, H, S, dk = q.shape
    dv = v.shape[-1]
    n_chunks = S // CHUNK

    def to_chunks(t):
        return t.reshape(B, H, n_chunks, CHUNK, t.shape[-1]).transpose(2, 0, 1, 3, 4)

    mask = jnp.tril(jnp.ones((CHUNK, CHUNK), dtype=bool))

    def step(state, inp):
        qc, kc, vc, lc = inp
        b = jnp.cumsum(lc, axis=-2)
        o_inter = jnp.einsum('bhtk,bhkv->bhtv', qc * jnp.exp(b), state)
        diff = b[:, :, :, None, :] - b[:, :, None, :, :]
        decay = jnp.exp(jnp.where(mask[:, :, None], diff, -jnp.inf))
        scores = jnp.einsum('bhtk,bhsk,bhtsk->bhts', qc, kc, decay)
        o_intra = jnp.einsum('bhts,bhsv->bhtv', scores, vc)
        b_last = b[:, :, -1:, :]
        new_state = (jnp.exp(b_last[:, :, 0, :])[..., None] * state
                     + jnp.einsum('bhsk,bhsv->bhkv', kc * jnp.exp(b_last - b), vc))
        return new_state, o_inter + o_intra

    init = jnp.zeros((B, H, dk, dv), jnp.float32)
    _, o = lax.scan(step, init, (to_chunks(q), to_chunks(k), to_chunks(v), to_chunks(logf)))
    return o.transpose(1, 2, 0, 3, 4).reshape(B, H, S, dv)


def hgrn2_mixer(q_raw, ff_raw, fb_raw, i_raw, g_raw, lb, norm_w):
    B, S, _ = q_raw.shape

    def heads(t):
        return t.reshape(B, S, HA_HEADS, HA_HEAD).transpose(0, 2, 1, 3).astype(jnp.float32)

    q = jax.nn.silu(heads(q_raw))
    v = heads(i_raw)

    def gates(x_raw, lb_dir):
        xg = heads(x_raw)
        lbh = lb_dir.astype(jnp.float32).reshape(HA_HEADS, 1, HA_HEAD)
        logf = jnp.logaddexp(jnp.log(lbh), jnp.log1p(-lbh) + jax.nn.log_sigmoid(xg))
        kk = (1.0 - lbh) * jax.nn.sigmoid(-xg)
        return kk, logf

    k_f, logf_f = gates(ff_raw, lb[0])
    k_b, logf_b = gates(fb_raw, lb[1])
    flip = lambda t: jnp.flip(t, axis=2)
    o_fwd = gla_chunk(q, k_f, v, logf_f)
    o_bwd = flip(gla_chunk(flip(q), flip(k_b), flip(v), flip(logf_b)))
    o = rmsnorm(o_fwd + o_bwd, norm_w) * jax.nn.silu(heads(g_raw))
    return o.transpose(0, 2, 1, 3).reshape(B, S, HA_WIDTH).astype(q_raw.dtype)


def diff_attention(q_raw, k_raw, v_raw, q_norm_w, k_norm_w, lam, lambda_init, subln_w, rel_bias):
    B, S, _ = q_raw.shape
    q = rmsnorm(q_raw.reshape(B, S, DB_HEADS, 2, DB_HEAD), q_norm_w)
    k = rmsnorm(k_raw.reshape(B, S, DB_HEADS, 2, DB_HEAD), k_norm_w)
    v = v_raw.reshape(B, S, DB_HEADS, 2 * DB_HEAD)
    n_blk = S // Q_BLOCK
    q_blocks = q.reshape(B, n_blk, Q_BLOCK, DB_HEADS, 2, DB_HEAD).transpose(1, 0, 2, 3, 4, 5)
    k_pos = jnp.arange(S, dtype=jnp.int32)
    scale = DB_HEAD ** -0.5

    def one_block(args):
        qb, blk = args
        q_pos = blk * Q_BLOCK + jnp.arange(Q_BLOCK, dtype=jnp.int32)
        bucket = rel_bucket(k_pos[None, :] - q_pos[:, None])
        bias = rel_bias[bucket].astype(jnp.float32).transpose(2, 0, 1)
        logits = (jnp.einsum('bqhcd,bkhcd->bhcqk', qb, k).astype(jnp.float32) * scale
                  + bias[None, :, None])
        p = jax.nn.softmax(logits, axis=-1)
        a = p[:, :, 0] - lam * p[:, :, 1]
        return jnp.einsum('bhqk,bkhe->bqhe', a.astype(v.dtype), v)

    out = lax.map(one_block, (q_blocks, jnp.arange(n_blk, dtype=jnp.int32)))
    out = out.transpose(1, 0, 2, 3, 4).reshape(B, S, DB_HEADS, 2 * DB_HEAD)
    out = rmsnorm(out, subln_w) * (1.0 - lambda_init)
    return out.reshape(B, S, DB_V_WIDTH).astype(q_raw.dtype)


def dwconv3(u, w, b):
    up = jnp.pad(u, ((0, 0), (1, 1), (0, 0)))
    return up[:, :-2] * w[0] + up[:, 1:-1] * w[1] + up[:, 2:] * w[2] + b


def trunk(x, c, w_ada, b_ada, norm_attn_w, w_in, hgrn_lb, hgrn_norm_w, q_norm_w, k_norm_w,
          diff_lambda, diff_subln_w, rel_bias, w_out, norm_ffn_w, w_up, conv_w, conv_b, w_down):
    lb_cum = jnp.cumsum(jax.nn.softmax(hgrn_lb.astype(jnp.float32), axis=0), axis=0)
    lb_all = lb_cum - lb_cum[0:1]
    sizes = [HA_WIDTH] * 5 + [DB_QK_WIDTH, DB_QK_WIDTH, DB_V_WIDTH]
    offsets = [int(o) for o in np.cumsum(sizes)[:-1]]
    for l in range(DEPTH):
        mod = (jax.nn.silu(c) @ w_ada[l] + b_ada[l])[:, None, :]
        shift1, scale1, gate1, shift2, scale2, gate2 = jnp.split(mod, 6, axis=-1)

        h = rmsnorm(x, norm_attn_w[l]) * (1 + scale1) + shift1
        z = h @ w_in[l]
        a_q, a_ff, a_fb, a_i, a_g, b_q, b_k, b_v = jnp.split(z, offsets, axis=-1)
        out_a = hgrn2_mixer(a_q, a_ff, a_fb, a_i, a_g, lb_all[l], hgrn_norm_w[l])
        lambda_init = 0.8 - 0.6 * math.exp(-0.3 * l)
        lp = diff_lambda[l].astype(jnp.float32)
        lam = (jnp.exp(jnp.sum(lp[0] * lp[1])) - jnp.exp(jnp.sum(lp[2] * lp[3]))
               + lambda_init)
        out_b = diff_attention(b_q, b_k, b_v, q_norm_w[l], k_norm_w[l], lam, lambda_init,
                               diff_subln_w[l], rel_bias)
        mix = jnp.concatenate([out_a, out_b], axis=-1) @ w_out[l]
        x = x + gate1 * mix

        h = rmsnorm(x, norm_ffn_w[l]) * (1 + scale2) + shift2
        u = dwconv3(h @ w_up[l], conv_w[l], conv_b[l])
        u_val, u_gate = jnp.split(u, 2, axis=-1)
        x = x + gate2 * ((jax.nn.silu(u_gate) * u_val) @ w_down[l])
    return x


def setup_inputs(seed: int = 0) -> dict:
    key = jax.random.key(seed)
    ks = jax.random.split(key, 24)
    D, F = D_MODEL, D_FF
    nrm = jax.random.normal
    return {
        "x_prompt": nrm(ks[0], (BATCH, SEQ, D), jnp.float32),
        "x_sample": nrm(ks[1], (DEC_BATCH, DEC_SEQ, D), jnp.float32),
        "c_prompt": nrm(ks[2], (BATCH, D), jnp.float32),
        "c_sample": nrm(ks[3], (DEC_BATCH, D), jnp.float32),
        "w_ada": nrm(ks[4], (DEPTH, D, 6 * D), jnp.float32) * D ** -0.5,
        "b_ada": nrm(ks[5], (DEPTH, 6 * D), jnp.float32) * 0.01,
        "norm_attn_w": 1.0 + 0.02 * nrm(ks[6], (DEPTH, D), jnp.float32),
        "w_in": nrm(ks[7], (DEPTH, D, IN_COLS), jnp.float32) * D ** -0.5,
        "hgrn_lb": nrm(ks[8], (DEPTH, 2, HA_WIDTH), jnp.float32),
        "hgrn_norm_w": 1.0 + 0.02 * nrm(ks[9], (DEPTH, HA_HEAD), jnp.float32),
        "q_norm_w": 1.0 + 0.02 * nrm(ks[10], (DEPTH, DB_HEAD), jnp.float32),
        "k_norm_w": 1.0 + 0.02 * nrm(ks[11], (DEPTH, DB_HEAD), jnp.float32),
        "diff_lambda": 0.1 * nrm(ks[12], (DEPTH, 4, DB_HEAD), jnp.float32),
        "diff_subln_w": 1.0 + 0.02 * nrm(ks[13], (DEPTH, 2 * DB_HEAD), jnp.float32),
        "rel_bias": 0.5 * nrm(ks[14], (NUM_BUCKETS, DB_HEADS), jnp.float32),
        "w_out": nrm(ks[15], (DEPTH, MIX_WIDTH, D), jnp.float32) * MIX_WIDTH ** -0.5,
        "norm_ffn_w": 1.0 + 0.02 * nrm(ks[16], (DEPTH, D), jnp.float32),
        "w_up": nrm(ks[17], (DEPTH, D, 2 * F), jnp.float32) * D ** -0.5,
        "conv_w": nrm(ks[18], (DEPTH, CONV_WIDTH, 2 * F), jnp.float32) * CONV_WIDTH ** -0.5,
        "conv_b": 0.01 * nrm(ks[19], (DEPTH, 2 * F), jnp.float32),
        "w_down": nrm(ks[20], (DEPTH, F, D), jnp.float32) * F ** -0.5,
    }


def reference(x_prompt, x_sample, c_prompt, c_sample, w_ada, b_ada, norm_attn_w, w_in, hgrn_lb,
              hgrn_norm_w, q_norm_w, k_norm_w, diff_lambda, diff_subln_w, rel_bias, w_out,
              norm_ffn_w, w_up, conv_w, conv_b, w_down):
    y_prompt = trunk(x_prompt, c_prompt, w_ada, b_ada, norm_attn_w, w_in, hgrn_lb, hgrn_norm_w,
                     q_norm_w, k_norm_w, diff_lambda, diff_subln_w, rel_bias, w_out, norm_ffn_w,
                     w_up, conv_w, conv_b, w_down)
    y_sample = trunk(x_sample, c_sample, w_ada, b_ada, norm_attn_w, w_in, hgrn_lb, hgrn_norm_w,
                     q_norm_w, k_norm_w, diff_lambda, diff_subln_w, rel_bias, w_out, norm_ffn_w,
                     w_up, conv_w, conv_b, w_down)
    return (y_prompt, y_sample)
```

```python
import functools
import math

import jax
import jax.numpy as jnp
from jax import lax
from jax.experimental import pallas as pl
from jax.experimental.pallas import tpu as pltpu

D_MODEL = 4096
HA_HEADS = 16
HA_HEAD = 128
HA_WIDTH = HA_HEADS * HA_HEAD
DB_HEADS = 8
DB_HEAD = 128
DB_WIDTH = DB_HEADS * 2 * DB_HEAD
IN_COLS = 5 * HA_WIDTH + 3 * DB_WIDTH
CHUNK = 64
NUM_BUCKETS = 32
EPS = 1e-6
BUCKET_START = (0, 1, 2, 3, 4, 5, 6, 7, 8, 12, 16, 23, 32, 46, 64, 91)

LANES = 128
V7X_VMEM_BYTES = 64 * 1024 * 1024
VMEM_LIMIT = V7X_VMEM_BYTES - 8 * 1024 * 1024

LN2 = math.log(2.0)
SAFE_HALF_SPAN = 80.0


def _cparams(n_axes):
    return pltpu.CompilerParams(
        dimension_semantics=("arbitrary",) * n_axes, vmem_limit_bytes=VMEM_LIMIT
    )


def _silu(x):
    return x * jax.nn.sigmoid(x)


def _ada_kernel(c_ref, w_ref, b_ref, o_ref):
    @pl.when(pl.program_id(2) == 0)
    def _():
        o_ref[...] = jnp.broadcast_to(b_ref[...], o_ref.shape)

    a = _silu(c_ref[...]).astype(jnp.bfloat16)
    o_ref[...] += jnp.dot(a, w_ref[...].astype(jnp.bfloat16), preferred_element_type=jnp.float32)


def adaln_mod(c_all, w_ada, b_ada, *, tk=1024, tn=2048):
    R, D = c_all.shape
    L, _, N = w_ada.shape
    return pl.pallas_call(
        _ada_kernel,
        grid=(L, N // tn, D // tk),
        in_specs=[
            pl.BlockSpec((R, tk), lambda l, j, k: (0, k)),
            pl.BlockSpec((None, tk, tn), lambda l, j, k: (l, k, j)),
            pl.BlockSpec((None, 1, tn), lambda l, j, k: (l, 0, j)),
        ],
        out_specs=pl.BlockSpec((None, R, tn), lambda l, j, k: (l, 0, j)),
        out_shape=jax.ShapeDtypeStruct((L, R, N), jnp.float32),
        compiler_params=_cparams(3),
        name="adaln_mod",
    )(c_all, w_ada, b_ada.reshape(L, 1, N))


def _normmod_kernel(x_ref, w_ref, sc_ref, sh_ref, o_ref):
    x = x_ref[...]
    ms = jnp.mean(x * x, axis=-1, keepdims=True)
    y = x * lax.rsqrt(ms + EPS) * w_ref[...]
    o_ref[...] = (y * (1.0 + sc_ref[...]) + sh_ref[...]).astype(o_ref.dtype)


def norm_modulate(x, w, mod, shift_idx, scale_idx, *, ts=512):
    B, S, D = x.shape
    ts = min(ts, S)
    return pl.pallas_call(
        _normmod_kernel,
        grid=(B, S // ts),
        in_specs=[
            pl.BlockSpec((None, ts, D), lambda b, i: (b, i, 0)),
            pl.BlockSpec((1, D), lambda b, i: (0, 0)),
            pl.BlockSpec((None, None, 1, D), lambda b, i: (b, scale_idx, 0, 0)),
            pl.BlockSpec((None, None, 1, D), lambda b, i: (b, shift_idx, 0, 0)),
        ],
        out_specs=pl.BlockSpec((None, ts, D), lambda b, i: (b, i, 0)),
        out_shape=jax.ShapeDtypeStruct((B, S, D), jnp.bfloat16),
        compiler_params=_cparams(2),
        name="norm_modulate",
    )(x, w.reshape(1, D), mod, mod)


def _mm_kernel(a_ref, w_ref, o_ref):
    o_ref[...] = jnp.dot(a_ref[...], w_ref[...], preferred_element_type=jnp.float32).astype(o_ref.dtype)


def matmul(a, w, out_dtype, *, tm, tn):
    B, S, K = a.shape
    N = w.shape[1]
    tm, tn = min(tm, S), min(tn, N)
    return pl.pallas_call(
        _mm_kernel,
        grid=(B, S // tm, N // tn),
        in_specs=[
            pl.BlockSpec((None, tm, K), lambda b, i, j: (b, i, 0)),
            pl.BlockSpec((K, tn), lambda b, i, j: (0, j)),
        ],
        out_specs=pl.BlockSpec((None, tm, tn), lambda b, i, j: (b, i, j)),
        out_shape=jax.ShapeDtypeStruct((B, S, N), out_dtype),
        compiler_params=_cparams(3),
        name="matmul",
    )(a, w)


def _mm_res_kernel(*refs, n_lhs):
    a_refs = refs[:n_lhs]
    w_ref, x_ref, g_ref, o_ref = refs[n_lhs:]
    acc = None
    k0 = 0
    for a_ref in a_refs:
        kw = a_ref.shape[-1]
        part = jnp.dot(a_ref[...], w_ref[k0:k0 + kw, :], preferred_element_type=jnp.float32)
        acc = part if acc is None else acc + part
        k0 += kw
    o_ref[...] = x_ref[...] + g_ref[...] * acc


def matmul_residual(lhs_parts, w, x, mod, gate_idx, *, tm, tn):
    B, S, N = x.shape
    K = w.shape[0]
    tm, tn = min(tm, S), min(tn, N)
    n_lhs = len(lhs_parts)
    in_specs = [pl.BlockSpec((None, tm, a.shape[-1]), lambda b, i, j: (b, i, 0)) for a in lhs_parts]
    in_specs += [
        pl.BlockSpec((K, tn), lambda b, i, j: (0, j)),
        pl.BlockSpec((None, tm, tn), lambda b, i, j: (b, i, j)),
        pl.BlockSpec((None, None, 1, tn), lambda b, i, j: (b, gate_idx, 0, j)),
    ]
    return pl.pallas_call(
        functools.partial(_mm_res_kernel, n_lhs=n_lhs),
        grid=(B, S // tm, N // tn),
        in_specs=in_specs,
        out_specs=pl.BlockSpec((None, tm, tn), lambda b, i, j: (b, i, j)),
        out_shape=jax.ShapeDtypeStruct((B, S, N), jnp.float32),
        compiler_params=_cparams(3),
        name="matmul_residual",
    )(*lhs_parts, w, x, mod)


def _convgate_kernel(uv_ref, ug_ref, wv_ref, wg_ref, bv_ref, bg_ref, o_ref, *, R):
    S = uv_ref.shape[0]
    n = S // R
    rows = lax.broadcasted_iota(jnp.int32, (R, LANES), 0)

    def conv(u_ref, w_ref, b_ref, r0, i):
        cur = u_ref[pl.ds(r0, R), :]
        p0 = pl.multiple_of(jnp.maximum(r0 - 8, 0), 8)
        prow = jnp.where(i > 0, u_ref[pl.ds(p0, 8), :][7:8, :], 0.0)
        n0 = pl.multiple_of(jnp.minimum(r0 + R, S - 8), 8)
        nrow = jnp.where(i < n - 1, u_ref[pl.ds(n0, 8), :][0:1, :], 0.0)
        up = jnp.where(rows == 0, prow, pltpu.roll(cur, 1, 0))
        dn = jnp.where(rows == R - 1, nrow, pltpu.roll(cur, R - 1, 0))
        w = w_ref[...]
        return up * w[0:1, :] + cur * w[1:2, :] + dn * w[2:3, :] + b_ref[...]

    def body(i, carry):
        r0 = pl.multiple_of(i * R, R)
        val = conv(uv_ref, wv_ref, bv_ref, r0, i)
        gate = conv(ug_ref, wg_ref, bg_ref, r0, i)
        o_ref[pl.ds(r0, R), :] = (_silu(gate) * val).astype(o_ref.dtype)
        return carry

    lax.fori_loop(0, n, body, 0)


def conv_gate(u, conv_w, conv_b, *, R=512):
    B, S, F2 = u.shape
    F = F2 // 2
    nf = F // LANES
    R = min(R, S)
    return pl.pallas_call(
        functools.partial(_convgate_kernel, R=R),
        grid=(B, nf),
        in_specs=[
            pl.BlockSpec((None, S, LANES), lambda b, j: (b, 0, j)),
            pl.BlockSpec((None, S, LANES), lambda b, j: (b, 0, nf + j)),
            pl.BlockSpec((3, LANES), lambda b, j: (0, j)),
            pl.BlockSpec((3, LANES), lambda b, j: (0, nf + j)),
            pl.BlockSpec((1, LANES), lambda b, j: (0, j)),
            pl.BlockSpec((1, LANES), lambda b, j: (0, nf + j)),
        ],
        out_specs=pl.BlockSpec((None, S, LANES), lambda b, j: (b, 0, j)),
        out_shape=jax.ShapeDtypeStruct((B, S, F), jnp.bfloat16),
        compiler_params=_cparams(2),
        name="conv_gate",
    )(u, u, conv_w, conv_w, conv_b.reshape(1, F2), conv_b.reshape(1, F2))


def _hgrn_kernel(*refs, reverse, T):
    if reverse:
        xq_ref, xf_ref, xi_ref, xg_ref, of_ref, llb_ref, l1m_ref, oml_ref, nw_ref, o_ref, st_ref = refs
    else:
        xq_ref, xf_ref, xi_ref, llb_ref, l1m_ref, oml_ref, o_ref, st_ref = refs
    C = CHUNK
    nC = T // C
    half = C // 2

    @pl.when(pl.program_id(2) == 0)
    def _():
        st_ref[...] = jnp.zeros_like(st_ref)

    span = jnp.maximum(-xf_ref[...], 0.0) + LN2
    span = jnp.sum(span.reshape(T // half, half, LANES), axis=1)
    exact_needed = jnp.max(span) > SAFE_HALF_SPAN

    row = lax.broadcasted_iota(jnp.int32, (C, LANES), 0)
    ti = lax.broadcasted_iota(jnp.int32, (C, C), 0)
    si = lax.broadcasted_iota(jnp.int32, (C, C), 1)
    causal = (si >= ti) if reverse else (si <= ti)
    llb = llb_ref[...]
    l1m = l1m_ref[...]
    oml = oml_ref[...]

    def chunk(ci, carry):
        cc = (nC - 1 - ci) if reverse else ci
        r0 = pl.multiple_of(cc * C, C)
        xq = xq_ref[pl.ds(r0, C), :]
        xf = xf_ref[pl.ds(r0, C), :]
        v = xi_ref[pl.ds(r0, C), :]
        q = _silu(xq)
        e = jnp.exp(-jnp.abs(xf))
        y = l1m + jnp.minimum(xf, 0.0) - jnp.log1p(e)
        logf = jnp.maximum(llb, y) + jnp.log1p(jnp.exp(-jnp.abs(llb - y)))
        kk = oml * jnp.where(xf >= 0.0, e, 1.0) / (1.0 + e)

        b = logf
        for sft in (1, 2, 4, 8, 16, 32):
            if reverse:
                b = b + jnp.where(row < C - sft, pltpu.roll(b, C - sft, 0), 0.0)
            else:
                b = b + jnp.where(row >= sft, pltpu.roll(b, sft, 0), 0.0)
        if reverse:
            b_last, b_mid = b[0:1, :], b[half:half + 1, :]
        else:
            b_last, b_mid = b[C - 1:C, :], b[half - 1:half, :]

        v16 = v.astype(jnp.bfloat16)

        def intra_factored():
            qm = q * jnp.exp(b - b_mid)
            km = kk * jnp.exp(b_mid - b)
            s = lax.dot_general(qm.astype(jnp.bfloat16), km.astype(jnp.bfloat16),
                                (((1,), (1,)), ((), ())), preferred_element_type=jnp.float32)
            s = jnp.where(causal, s, 0.0)
            return jnp.dot(s.astype(jnp.bfloat16), v16, preferred_element_type=jnp.float32)

        def intra_exact():
            def pair(d, acc):
                sh = ((C - d) % C) if reverse else d
                valid = (row < C - d) if reverse else (row >= d)
                kd = pltpu.roll(kk, sh, 0)
                bd = pltpu.roll(b, sh, 0)
                vd = pltpu.roll(v, sh, 0)
                dec = jnp.exp(jnp.where(valid, b - bd, -jnp.inf))
                w = jnp.sum(q * kd * dec, axis=-1, keepdims=True)
                return acc + w * vd

            return lax.fori_loop(0, C, pair, jnp.zeros((C, LANES), jnp.float32))

        o_intra = lax.cond(exact_needed, intra_exact, intra_factored)

        st = st_ref[...]
        qa = (q * jnp.exp(b)).astype(jnp.bfloat16)
        o = o_intra + lax.dot_general(qa, st.astype(jnp.bfloat16), (((1,), (1,)), ((), ())),
                                      preferred_element_type=jnp.float32)
        kl = (kk * jnp.exp(b_last - b)).astype(jnp.bfloat16)
        upd = lax.dot_general(v16, kl, (((0,), (0,)), ((), ())), preferred_element_type=jnp.float32)
        st_ref[...] = st * jnp.exp(b_last) + upd

        if reverse:
            tot = of_ref[pl.ds(r0, C), :] + o
            ms = jnp.mean(tot * tot, axis=-1, keepdims=True)
            yn = tot * lax.rsqrt(ms + EPS) * nw_ref[...]
            o_ref[pl.ds(r0, C), :] = (yn * _silu(xg_ref[pl.ds(r0, C), :])).astype(o_ref.dtype)
        else:
            o_ref[pl.ds(r0, C), :] = o
        return carry

    lax.fori_loop(0, nC, chunk, 0)


def hgrn_pass(z, gate_params, *, reverse, o_fwd=None, norm_w=None, T=512):
    B, S, _ = z.shape
    T = min(T, S)
    nblk = S // T
    H = HA_HEADS

    def seq(c):
        return (nblk - 1 - c) if reverse else c

    def zspec(section):
        return pl.BlockSpec((None, T, LANES), lambda b, h, c: (b, seq(c), section * H + h))

    pspec = pl.BlockSpec((1, LANES), lambda b, h, c: (0, h))
    llb, l1m, oml = gate_params
    if reverse:
        in_specs = [zspec(0), zspec(2), zspec(3), zspec(4),
                    pl.BlockSpec((None, T, LANES), lambda b, h, c: (b, seq(c), h)),
                    pspec, pspec, pspec, pl.BlockSpec((1, LANES), lambda b, h, c: (0, 0))]
        args = (z, z, z, z, o_fwd, llb, l1m, oml, norm_w.reshape(1, LANES))
        out_dtype = jnp.bfloat16
    else:
        in_specs = [zspec(0), zspec(1), zspec(3), pspec, pspec, pspec]
        args = (z, z, z, llb, l1m, oml)
        out_dtype = jnp.float32
    return pl.pallas_call(
        functools.partial(_hgrn_kernel, reverse=reverse, T=T),
        grid=(B, H, nblk),
        in_specs=in_specs,
        out_specs=pl.BlockSpec((None, T, LANES), lambda b, h, c: (b, seq(c), h)),
        out_shape=jax.ShapeDtypeStruct((B, S, HA_WIDTH), out_dtype),
        scratch_shapes=[pltpu.VMEM((HA_HEAD, HA_HEAD), jnp.float32)],
        compiler_params=_cparams(3),
        name="hgrn_bwd" if reverse else "hgrn_fwd",
    )(*args)


def _qknorm_kernel(q_ref, k_ref, v_ref, qw_ref, kw_ref, qo_ref, ko_ref, vo_ref):
    for g in range(DB_WIDTH // DB_HEAD):
        sl = slice(g * DB_HEAD, (g + 1) * DB_HEAD)
        for x_ref, w_ref, o_ref in ((q_ref, qw_ref, qo_ref), (k_ref, kw_ref, ko_ref)):
            x = x_ref[:, sl]
            ms = jnp.mean(x * x, axis=-1, keepdims=True)
            o_ref[:, sl] = (x * lax.rsqrt(ms + EPS) * w_ref[...]).astype(o_ref.dtype)
    vo_ref[...] = v_ref[...].astype(vo_ref.dtype)


def qk_norm(z, q_norm_w, k_norm_w, *, ts=256):
    B, S, _ = z.shape
    ts = min(ts, S)
    first = 5 * HA_WIDTH // DB_WIDTH

    def zspec(sec):
        return pl.BlockSpec((None, ts, DB_WIDTH), lambda b, i: (b, i, first + sec))

    wspec = pl.BlockSpec((1, DB_HEAD), lambda b, i: (0, 0))
    ospec = pl.BlockSpec((None, ts, DB_WIDTH), lambda b, i: (b, i, 0))
    oshape = jax.ShapeDtypeStruct((B, S, DB_WIDTH), jnp.bfloat16)
    return pl.pallas_call(
        _qknorm_kernel,
        grid=(B, S // ts),
        in_specs=[zspec(0), zspec(1), zspec(2), wspec, wspec],
        out_specs=(ospec, ospec, ospec),
        out_shape=(oshape, oshape, oshape),
        compiler_params=_cparams(2),
        name="qk_norm",
    )(z, z, z, q_norm_w.reshape(1, DB_HEAD), k_norm_w.reshape(1, DB_HEAD))


def _bias_kernel(rb_ref, o_ref, *, t):
    h = pl.program_id(0)
    offset = (pl.program_id(1) - 2) * t
    qi = lax.broadcasted_iota(jnp.int32, (t, t), 0)
    kj = lax.broadcasted_iota(jnp.int32, (t, t), 1)
    rel = kj - qi + offset
    n = jnp.abs(rel)
    half = NUM_BUCKETS // 2

    def side(base):
        val = jnp.full((t, t), rb_ref[base + half - 1, h], jnp.float32)
        for c in range(half - 2, -1, -1):
            val = jnp.where(n < BUCKET_START[c + 1], rb_ref[base + c, h], val)
        return val

    o_ref[...] = jnp.where(rel > 0, side(half), side(0))


def rel_bias_tiles(rel_bias, t):
    assert t >= 128
    return pl.pallas_call(
        functools.partial(_bias_kernel, t=t),
        grid=(DB_HEADS, 5),
        in_specs=[pl.BlockSpec(memory_space=pltpu.SMEM)],
        out_specs=pl.BlockSpec((None, None, t, t), lambda h, d: (h, d, 0, 0)),
        out_shape=jax.ShapeDtypeStruct((DB_HEADS, 5, t, t), jnp.float32),
        compiler_params=_cparams(2),
        name="rel_bias_tiles",
    )(rel_bias)


def _attn_kernel(q_ref, k_ref, v_ref, bias_ref, lp_ref, sw_ref, o_ref, m_ref, l_ref, acc_ref, *, lambda_init):
    kb = pl.program_id(3)

    @pl.when(kb == 0)
    def _():
        m_ref[...] = jnp.full_like(m_ref, -jnp.inf)
        l_ref[...] = jnp.zeros_like(l_ref)
        acc_ref[...] = jnp.zeros_like(acc_ref)

    bias = bias_ref[...]
    v = v_ref[...]
    scale = DB_HEAD ** -0.5
    for c in range(2):
        sl = slice(c * DB_HEAD, (c + 1) * DB_HEAD)
        s = lax.dot_general(q_ref[:, sl], k_ref[:, sl], (((1,), (1,)), ((), ())),
                            preferred_element_type=jnp.float32) * scale + bias
        m_prev = m_ref[c]
        m_new = jnp.maximum(m_prev, jnp.max(s, axis=-1, keepdims=True))
        alpha = jnp.exp(m_prev - m_new)
        p = jnp.exp(s - m_new)
        l_ref[c] = alpha * l_ref[c] + jnp.sum(p, axis=-1, keepdims=True)
        acc_ref[c] = alpha * acc_ref[c] + jnp.dot(p.astype(jnp.bfloat16), v, preferred_element_type=jnp.float32)
        m_ref[c] = m_new

    @pl.when(kb == pl.num_programs(3) - 1)
    def _():
        lp = lp_ref[...]
        lam = (jnp.exp(jnp.sum(lp[0:1, :] * lp[1:2, :], axis=-1, keepdims=True))
               - jnp.exp(jnp.sum(lp[2:3, :] * lp[3:4, :], axis=-1, keepdims=True)) + lambda_init)
        o = acc_ref[0] / l_ref[0] - lam * (acc_ref[1] / l_ref[1])
        ms = jnp.mean(o * o, axis=-1, keepdims=True)
        o_ref[...] = (o * lax.rsqrt(ms + EPS) * sw_ref[...] * (1.0 - lambda_init)).astype(o_ref.dtype)


def diff_attention(qn, kn, vb, bias_tiles, lam_params, subln_w, lambda_init, *, t):
    B, S, _ = qn.shape
    nb = S // t
    W = 2 * DB_HEAD

    def bias_idx(b, h, i, j):
        return (h, jnp.clip(j - i, -2, 2) + 2, 0, 0)

    return pl.pallas_call(
        functools.partial(_attn_kernel, lambda_init=lambda_init),
        grid=(B, DB_HEADS, nb, nb),
        in_specs=[
            pl.BlockSpec((None, t, W), lambda b, h, i, j: (b, i, h)),
            pl.BlockSpec((None, t, W), lambda b, h, i, j: (b, j, h)),
            pl.BlockSpec((None, t, W), lambda b, h, i, j: (b, j, h)),
            pl.BlockSpec((None, None, t, t), bias_idx),
            pl.BlockSpec((4, DB_HEAD), lambda b, h, i, j: (0, 0)),
            pl.BlockSpec((1, W), lambda b, h, i, j: (0, 0)),
        ],
        out_specs=pl.BlockSpec((None, t, W), lambda b, h, i, j: (b, i, h)),
        out_shape=jax.ShapeDtypeStruct((B, S, DB_WIDTH), jnp.bfloat16),
        scratch_shapes=[pltpu.VMEM((2, t, 1), jnp.float32), pltpu.VMEM((2, t, 1), jnp.float32),
                        pltpu.VMEM((2, t, W), jnp.float32)],
        compiler_params=_cparams(4),
        name="diff_attention",
    )(qn, kn, vb, bias_tiles, lam_params, subln_w.reshape(1, W))


ATTN_TILE = 512


def _trunk(x, mod, layers, rel_bias):
    attn_tile = min(ATTN_TILE, x.shape[1])
    bias_tiles = rel_bias_tiles(rel_bias, attn_tile)
    for l, p in enumerate(layers):
        m = mod[l]
        lambda_init = 0.8 - 0.6 * math.exp(-0.3 * l)
        h = norm_modulate(x, p["norm_attn_w"], m, 0, 1)
        z = matmul(h, p["w_in"], jnp.float32, tm=1024, tn=1024)
        o_fwd = hgrn_pass(z, p["gates_fwd"], reverse=False)
        out_a = hgrn_pass(z, p["gates_bwd"], reverse=True, o_fwd=o_fwd, norm_w=p["hgrn_norm_w"])
        qn, kn, vb = qk_norm(z, p["q_norm_w"], p["k_norm_w"])
        out_b = diff_attention(qn, kn, vb, bias_tiles, p["diff_lambda"], p["diff_subln_w"], lambda_init,
                               t=attn_tile)
        x = matmul_residual([out_a, out_b], p["w_out"], x, m, 2, tm=1024, tn=1024)
        h = norm_modulate(x, p["norm_ffn_w"], m, 3, 4)
        u = matmul(h, p["w_up"], jnp.float32, tm=1024, tn=512)
        act = conv_gate(u, p["conv_w"], p["conv_b"])
        x = matmul_residual([act], p["w_down"], x, m, 5, tm=512, tn=512)
    return x


def kernel(x_prompt, x_sample, c_prompt, c_sample, w_ada, b_ada, norm_attn_w, w_in, hgrn_lb, hgrn_norm_w,
           q_norm_w, k_norm_w, diff_lambda, diff_subln_w, rel_bias, w_out, norm_ffn_w, w_up, conv_w, conv_b,
           w_down):
    L = w_ada.shape[0]
    D = x_prompt.shape[-1]
    Bp, Bs = c_prompt.shape[0], c_sample.shape[0]
    rows = -(-(Bp + Bs) // 8) * 8
    c_all = jnp.zeros((rows, D), jnp.float32).at[:Bp].set(c_prompt).at[Bp:Bp + Bs].set(c_sample)
    mod = adaln_mod(c_all, w_ada, b_ada)
    mod_p = mod[:, :Bp].reshape(L, Bp, 6, 1, D)
    mod_s = mod[:, Bp:Bp + Bs].reshape(L, Bs, 6, 1, D)

    lb_cum = jnp.cumsum(jax.nn.softmax(hgrn_lb.astype(jnp.float32), axis=0), axis=0)
    lb_all = lb_cum - lb_cum[0:1]
    log_lb, log_1m_lb, one_m_lb = jnp.log(lb_all), jnp.log1p(-lb_all), 1.0 - lb_all

    layers = []
    for l in range(L):
        def gates(d):
            return (log_lb[l, d][None], log_1m_lb[l, d][None], one_m_lb[l, d][None])
        layers.append(dict(
            norm_attn_w=norm_attn_w[l], w_in=w_in[l].astype(jnp.bfloat16),
            gates_fwd=gates(0), gates_bwd=gates(1), hgrn_norm_w=hgrn_norm_w[l],
            q_norm_w=q_norm_w[l], k_norm_w=k_norm_w[l], diff_lambda=diff_lambda[l].astype(jnp.float32),
            diff_subln_w=diff_subln_w[l], w_out=w_out[l].astype(jnp.bfloat16), norm_ffn_w=norm_ffn_w[l],
            w_up=w_up[l].astype(jnp.bfloat16), conv_w=conv_w[l], conv_b=conv_b[l],
            w_down=w_down[l].astype(jnp.bfloat16)))
    rel_bias = rel_bias.astype(jnp.float32)
    y_prompt = _trunk(x_prompt, mod_p, layers, rel_bias)
    y_sample = _trunk(x_sample, mod_s, layers, rel_bias)
    return (y_prompt, y_sample)
```

```python
import functools
import math

import jax
import jax.numpy as jnp
from jax import lax
from jax.experimental import pallas as pl
from jax.experimental.pallas import tpu as pltpu

D_MODEL = 4096
HA_HEADS = 16
HA_HEAD = 128
HA_WIDTH = HA_HEADS * HA_HEAD
DB_HEADS = 8
DB_HEAD = 128
DB_WIDTH = DB_HEADS * 2 * DB_HEAD
IN_COLS = 5 * HA_WIDTH + 3 * DB_WIDTH
CHUNK = 64
NUM_BUCKETS = 32
EPS = 1e-6
BUCKET_START = (0, 1, 2, 3, 4, 5, 6, 7, 8, 12, 16, 23, 32, 46, 64, 91)

LANES = 128
V7X_VMEM_BYTES = 64 * 1024 * 1024
VMEM_LIMIT = V7X_VMEM_BYTES - 8 * 1024 * 1024

LN2 = math.log(2.0)
LOG2E = 1.0 / LN2
SAFE_HALF_SPAN = 80.0

_NT = (((1,), (1,)), ((), ()))
_TN = (((0,), (0,)), ((), ()))


def _cparams(n_axes):
    return pltpu.CompilerParams(
        dimension_semantics=("arbitrary",) * n_axes, vmem_limit_bytes=VMEM_LIMIT
    )


def _silu(x):
    return x * jax.nn.sigmoid(x)


def _ada_kernel(c_ref, w_ref, b_ref, o_ref):
    @pl.when(pl.program_id(2) == 0)
    def _():
        o_ref[...] = jnp.broadcast_to(b_ref[...], o_ref.shape)

    a = _silu(c_ref[...]).astype(jnp.bfloat16)
    o_ref[...] += jnp.dot(a, w_ref[...].astype(jnp.bfloat16), preferred_element_type=jnp.float32)


def adaln_mod(c_all, w_ada, b_ada, *, tk=1024, tn=2048):
    R, D = c_all.shape
    L, _, N = w_ada.shape
    return pl.pallas_call(
        _ada_kernel,
        grid=(L, N // tn, D // tk),
        in_specs=[
            pl.BlockSpec((R, tk), lambda l, j, k: (0, k)),
            pl.BlockSpec((None, tk, tn), lambda l, j, k: (l, k, j)),
            pl.BlockSpec((None, 1, tn), lambda l, j, k: (l, 0, j)),
        ],
        out_specs=pl.BlockSpec((None, R, tn), lambda l, j, k: (l, 0, j)),
        out_shape=jax.ShapeDtypeStruct((L, R, N), jnp.float32),
        compiler_params=_cparams(3),
        name="adaln_mod",
    )(c_all, w_ada, b_ada.reshape(L, 1, N))


def _normmod_kernel(x_ref, w_ref, sc_ref, sh_ref, o_ref):
    x = x_ref[...]
    ms = jnp.mean(x * x, axis=-1, keepdims=True)
    y = x * lax.rsqrt(ms + EPS) * w_ref[...]
    o_ref[...] = (y * (1.0 + sc_ref[...]) + sh_ref[...]).astype(o_ref.dtype)


def norm_modulate(x, w, mod, shift_idx, scale_idx, *, ts=512):
    B, S, D = x.shape
    ts = min(ts, S)
    return pl.pallas_call(
        _normmod_kernel,
        grid=(B, S // ts),
        in_specs=[
            pl.BlockSpec((None, ts, D), lambda b, i: (b, i, 0)),
            pl.BlockSpec((1, D), lambda b, i: (0, 0)),
            pl.BlockSpec((None, None, 1, D), lambda b, i: (b, scale_idx, 0, 0)),
            pl.BlockSpec((None, None, 1, D), lambda b, i: (b, shift_idx, 0, 0)),
        ],
        out_specs=pl.BlockSpec((None, ts, D), lambda b, i: (b, i, 0)),
        out_shape=jax.ShapeDtypeStruct((B, S, D), jnp.bfloat16),
        compiler_params=_cparams(2),
        name="norm_modulate",
    )(x, w.reshape(1, D), mod, mod)


def _mm_grouped_kernel(a_ref, w_ref, o_ref):
    acc = jnp.dot(a_ref[...], w_ref[...], preferred_element_type=jnp.float32)
    for g in range(o_ref.shape[0]):
        o_ref[g] = acc[:, g * LANES:(g + 1) * LANES].astype(o_ref.dtype)


def matmul_grouped(a, w, out_dtype, *, tm, tn):
    B, S, K = a.shape
    N = w.shape[1]
    tm, tn = min(tm, S), min(tn, N)
    return pl.pallas_call(
        _mm_grouped_kernel,
        grid=(B, S // tm, N // tn),
        in_specs=[
            pl.BlockSpec((None, tm, K), lambda b, i, j: (b, i, 0)),
            pl.BlockSpec((K, tn), lambda b, i, j: (0, j)),
        ],
        out_specs=pl.BlockSpec((None, tn // LANES, tm, LANES), lambda b, i, j: (b, j, i, 0)),
        out_shape=jax.ShapeDtypeStruct((B, N // LANES, S, LANES), out_dtype),
        compiler_params=_cparams(3),
        name="matmul_grouped",
    )(a, w)


def _mm_res_kernel(*refs, n_lhs):
    a_refs = refs[:n_lhs]
    w_ref, x_ref, g_ref, o_ref = refs[n_lhs:]
    acc = None
    k0 = 0
    for a_ref in a_refs:
        kw = a_ref.shape[-1]
        part = jnp.dot(a_ref[...], w_ref[k0:k0 + kw, :], preferred_element_type=jnp.float32)
        acc = part if acc is None else acc + part
        k0 += kw
    o_ref[...] = x_ref[...] + g_ref[...] * acc


def matmul_residual(lhs_parts, w, x, mod, gate_idx, *, tm, tn):
    B, S, N = x.shape
    K = w.shape[0]
    tm, tn = min(tm, S), min(tn, N)
    n_lhs = len(lhs_parts)
    in_specs = [pl.BlockSpec((None, tm, a.shape[-1]), lambda b, i, j: (b, i, 0)) for a in lhs_parts]
    in_specs += [
        pl.BlockSpec((K, tn), lambda b, i, j: (0, j)),
        pl.BlockSpec((None, tm, tn), lambda b, i, j: (b, i, j)),
        pl.BlockSpec((None, None, 1, tn), lambda b, i, j: (b, gate_idx, 0, j)),
    ]
    return pl.pallas_call(
        functools.partial(_mm_res_kernel, n_lhs=n_lhs),
        grid=(B, S // tm, N // tn),
        in_specs=in_specs,
        out_specs=pl.BlockSpec((None, tm, tn), lambda b, i, j: (b, i, j)),
        out_shape=jax.ShapeDtypeStruct((B, S, N), jnp.float32),
        compiler_params=_cparams(3),
        name="matmul_residual",
    )(*lhs_parts, w, x, mod)


def _convgate_kernel(uv_ref, uvp_ref, uvn_ref, ug_ref, ugp_ref, ugn_ref, wv_ref, wg_ref, bv_ref, bg_ref, o_ref):
    i = pl.program_id(1)
    last = pl.num_programs(1) - 1
    gc, ts, _ = uv_ref.shape
    rows = lax.broadcasted_iota(jnp.int32, (ts, LANES), 0)

    def conv(g, cur_ref, prev_ref, next_ref, w_ref, b_ref):
        sl = slice(g * LANES, (g + 1) * LANES)
        cur = cur_ref[g]
        prow = jnp.where(i > 0, prev_ref[g][7:8, :], 0.0)
        nrow = jnp.where(i < last, next_ref[g][0:1, :], 0.0)
        up = jnp.where(rows == 0, prow, pltpu.roll(cur, 1, 0))
        dn = jnp.where(rows == ts - 1, nrow, pltpu.roll(cur, ts - 1, 0))
        return up * w_ref[0:1, sl] + cur * w_ref[1:2, sl] + dn * w_ref[2:3, sl] + b_ref[:, sl]

    for g in range(gc):
        val = conv(g, uv_ref, uvp_ref, uvn_ref, wv_ref, bv_ref)
        gate = conv(g, ug_ref, ugp_ref, ugn_ref, wg_ref, bg_ref)
        o_ref[:, g * LANES:(g + 1) * LANES] = (_silu(gate) * val).astype(o_ref.dtype)


def conv_gate(u, conv_w, conv_b, *, ts=1024, gc=2):
    B, G2, S, _ = u.shape
    nf = G2 // 2
    F = nf * LANES
    ts = min(ts, S)
    nj = nf // gc
    r8 = ts // 8

    def cur(off):
        return pl.BlockSpec((None, gc, ts, LANES), lambda b, i, j: (b, off + j, i, 0))

    def prev(off):
        return pl.BlockSpec((None, gc, 8, LANES), lambda b, i, j: (b, off + j, jnp.maximum(i * r8 - 1, 0), 0))

    def nxt(off):
        return pl.BlockSpec((None, gc, 8, LANES),
                            lambda b, i, j: (b, off + j, jnp.minimum((i + 1) * r8, S // 8 - 1), 0))

    def par(rows, off):
        return pl.BlockSpec((rows, gc * LANES), lambda b, i, j: (0, off + j))

    cb = conv_b.reshape(1, 2 * F)
    return pl.pallas_call(
        _convgate_kernel,
        grid=(B, S // ts, nj),
        in_specs=[cur(0), prev(0), nxt(0), cur(nj), prev(nj), nxt(nj), par(3, 0), par(3, nj), par(1, 0), par(1, nj)],
        out_specs=pl.BlockSpec((None, ts, gc * LANES), lambda b, i, j: (b, i, j)),
        out_shape=jax.ShapeDtypeStruct((B, S, F), jnp.bfloat16),
        compiler_params=_cparams(3),
        name="conv_gate",
    )(u, u, u, u, u, u, conv_w, conv_w, cb, cb)


def _hgrn_kernel(*refs, reverse, T):
    if reverse:
        (xq_ref, xf_ref, xi_ref, xg_ref, of_ref, llb_ref, l1m_ref, oml_ref, nw_ref, o_ref,
         st_ref, oi_ref, q_ref, k_ref, b_ref) = refs
    else:
        xq_ref, xf_ref, xi_ref, llb_ref, l1m_ref, oml_ref, o_ref, st_ref, oi_ref, q_ref, k_ref, b_ref = refs
    C = CHUNK
    nC = T // C
    half = C // 2

    @pl.when(pl.program_id(2) == 0)
    def _():
        st_ref[...] = jnp.zeros_like(st_ref)

    xf = xf_ref[...]
    span = jnp.maximum(-xf, 0.0) + LN2
    span = jnp.sum(span.reshape(T // half, half, LANES), axis=1)
    exact_needed = jnp.max(span) > SAFE_HALF_SPAN

    llb, l1m, oml = llb_ref[...], l1m_ref[...], oml_ref[...]
    q = _silu(xq_ref[...])
    v16 = xi_ref[...].astype(jnp.bfloat16)
    e = jnp.exp(-jnp.abs(xf))
    y = l1m + jnp.minimum(xf, 0.0) - jnp.log1p(e)
    logf = jnp.maximum(llb, y) + jnp.log1p(jnp.exp(-jnp.abs(llb - y)))
    kk = oml * jnp.where(xf >= 0.0, e, 1.0) / (1.0 + e)

    rc = lax.broadcasted_iota(jnp.int32, (T, LANES), 0) & (C - 1)
    b = logf
    for sft in (1, 2, 4, 8, 16, 32):
        if reverse:
            b = b + jnp.where(rc < C - sft, pltpu.roll(b, T - sft, 0), 0.0)
        else:
            b = b + jnp.where(rc >= sft, pltpu.roll(b, sft, 0), 0.0)

    def rows(x, c):
        return x[c * C:(c + 1) * C, :]

    def b_last(bc):
        return bc[0:1, :] if reverse else bc[C - 1:C, :]

    ti = lax.broadcasted_iota(jnp.int32, (C, C), 0)
    si = lax.broadcasted_iota(jnp.int32, (C, C), 1)
    causal = (si >= ti) if reverse else (si <= ti)

    @pl.when(jnp.logical_not(exact_needed))
    def _():
        for c in range(nC):
            bc = rows(b, c)
            b_mid = bc[half:half + 1, :] if reverse else bc[half - 1:half, :]
            qm = (rows(q, c) * jnp.exp(bc - b_mid)).astype(jnp.bfloat16)
            km = (rows(kk, c) * jnp.exp(b_mid - bc)).astype(jnp.bfloat16)
            s = lax.dot_general(qm, km, _NT, preferred_element_type=jnp.float32)
            s = jnp.where(causal, s, 0.0).astype(jnp.bfloat16)
            oi_ref[c * C:(c + 1) * C, :] = jnp.dot(s, rows(v16, c), preferred_element_type=jnp.float32)

    @pl.when(exact_needed)
    def _():
        q_ref[...] = q
        k_ref[...] = kk
        b_ref[...] = b
        row = lax.broadcasted_iota(jnp.int32, (C, LANES), 0)

        def per_chunk(c, carry):
            r0 = pl.multiple_of(c * C, C)
            qc = q_ref[pl.ds(r0, C), :]
            kc = k_ref[pl.ds(r0, C), :]
            bc = b_ref[pl.ds(r0, C), :]
            vc = xi_ref[pl.ds(r0, C), :]

            def pair(d, acc):
                sh = ((C - d) % C) if reverse else d
                valid = (row < C - d) if reverse else (row >= d)
                dec = jnp.exp(jnp.where(valid, bc - pltpu.roll(bc, sh, 0), -jnp.inf))
                w = jnp.sum(qc * pltpu.roll(kc, sh, 0) * dec, axis=-1, keepdims=True)
                return acc + w * pltpu.roll(vc, sh, 0)

            oi_ref[pl.ds(r0, C), :] = lax.fori_loop(0, C, pair, jnp.zeros((C, LANES), jnp.float32))
            return carry

        lax.fori_loop(0, nC, per_chunk, 0)

    st = st_ref[...]
    for c in (range(nC - 1, -1, -1) if reverse else range(nC)):
        bc = rows(b, c)
        bl = b_last(bc)
        qa = (rows(q, c) * jnp.exp(bc)).astype(jnp.bfloat16)
        o = oi_ref[c * C:(c + 1) * C, :] + lax.dot_general(qa, st.astype(jnp.bfloat16), _NT,
                                                            preferred_element_type=jnp.float32)
        kl = (rows(kk, c) * jnp.exp(bl - bc)).astype(jnp.bfloat16)
        st = st * jnp.exp(bl) + lax.dot_general(rows(v16, c), kl, _TN, preferred_element_type=jnp.float32)
        if reverse:
            tot = of_ref[c * C:(c + 1) * C, :] + o
            ms = jnp.mean(tot * tot, axis=-1, keepdims=True)
            yn = tot * lax.rsqrt(ms + EPS) * nw_ref[...]
            o_ref[c * C:(c + 1) * C, :] = (yn * _silu(xg_ref[c * C:(c + 1) * C, :])).astype(o_ref.dtype)
        else:
            o_ref[c * C:(c + 1) * C, :] = o
    st_ref[...] = st


def hgrn_pass(zg, gate_params, *, reverse, o_fwd=None, norm_w=None, T=512):
    B, _, S, _ = zg.shape
    T = min(T, S)
    nblk = S // T
    H = HA_HEADS

    def seq(c):
        return (nblk - 1 - c) if reverse else c

    def zspec(section):
        return pl.BlockSpec((None, None, T, LANES), lambda b, h, c: (b, section * H + h, seq(c), 0))

    hspec = pl.BlockSpec((None, None, T, LANES), lambda b, h, c: (b, h, seq(c), 0))
    pspec = pl.BlockSpec((1, LANES), lambda b, h, c: (0, h))
    llb, l1m, oml = gate_params
    if reverse:
        in_specs = [zspec(0), zspec(2), zspec(3), zspec(4), hspec,
                    pspec, pspec, pspec, pl.BlockSpec((1, LANES), lambda b, h, c: (0, 0))]
        args = (zg, zg, zg, zg, o_fwd, llb, l1m, oml, norm_w.reshape(1, LANES))
        out_spec = pl.BlockSpec((None, T, LANES), lambda b, h, c: (b, seq(c), h))
        out_shape = jax.ShapeDtypeStruct((B, S, HA_WIDTH), jnp.bfloat16)
    else:
        in_specs = [zspec(0), zspec(1), zspec(3), pspec, pspec, pspec]
        args = (zg, zg, zg, llb, l1m, oml)
        out_spec = hspec
        out_shape = jax.ShapeDtypeStruct((B, H, S, LANES), jnp.float32)
    blk = pltpu.VMEM((T, LANES), jnp.float32)
    return pl.pallas_call(
        functools.partial(_hgrn_kernel, reverse=reverse, T=T),
        grid=(B, H, nblk),
        in_specs=in_specs,
        out_specs=out_spec,
        out_shape=out_shape,
        scratch_shapes=[pltpu.VMEM((HA_HEAD, HA_HEAD), jnp.float32), blk, blk, blk, blk],
        compiler_params=_cparams(3),
        name="hgrn_bwd" if reverse else "hgrn_fwd",
    )(*args)


def _qknorm_kernel(q_ref, k_ref, v_ref, qw_ref, kw_ref, qo_ref, ko_ref, vo_ref, *, q_scale):
    qw = qw_ref[...] * q_scale
    kw = kw_ref[...]
    for g in range(q_ref.shape[0]):
        h, c = divmod(g, 2)
        sl = slice(c * DB_HEAD, (c + 1) * DB_HEAD)
        for x_ref, w, o_ref in ((q_ref, qw, qo_ref), (k_ref, kw, ko_ref)):
            x = x_ref[g]
            ms = jnp.mean(x * x, axis=-1, keepdims=True)
            o_ref[h, :, sl] = (x * lax.rsqrt(ms + EPS) * w).astype(o_ref.dtype)
        vo_ref[h, :, sl] = v_ref[g].astype(vo_ref.dtype)


def qk_norm(zg, q_norm_w, k_norm_w, *, ts=512):
    B, _, S, _ = zg.shape
    ts = min(ts, S)
    ng = DB_WIDTH // LANES
    first = 5 * HA_WIDTH // DB_WIDTH

    def zspec(sec):
        return pl.BlockSpec((None, ng, ts, LANES), lambda b, i: (b, first + sec, i, 0))

    wspec = pl.BlockSpec((1, DB_HEAD), lambda b, i: (0, 0))
    ospec = pl.BlockSpec((None, DB_HEADS, ts, 2 * DB_HEAD), lambda b, i: (b, 0, i, 0))
    oshape = jax.ShapeDtypeStruct((B, DB_HEADS, S, 2 * DB_HEAD), jnp.bfloat16)
    return pl.pallas_call(
        functools.partial(_qknorm_kernel, q_scale=DB_HEAD ** -0.5 * LOG2E),
        grid=(B, S // ts),
        in_specs=[zspec(0), zspec(1), zspec(2), wspec, wspec],
        out_specs=(ospec, ospec, ospec),
        out_shape=(oshape, oshape, oshape),
        compiler_params=_cparams(2),
        name="qk_norm",
    )(zg, zg, zg, q_norm_w.reshape(1, DB_HEAD), k_norm_w.reshape(1, DB_HEAD))


def _bias_kernel(rb_ref, o_ref, *, tq, tk):
    h = pl.program_id(0)
    offset = (pl.program_id(1) - (tk // tq + 1)) * tq
    qi = lax.broadcasted_iota(jnp.int32, (tq, tk), 0)
    kj = lax.broadcasted_iota(jnp.int32, (tq, tk), 1)
    rel = kj - qi + offset
    n = jnp.abs(rel)
    half = NUM_BUCKETS // 2

    def side(base):
        val = jnp.full((tq, tk), rb_ref[base + half - 1, h], jnp.float32)
        for c in range(half - 2, -1, -1):
            val = jnp.where(n < BUCKET_START[c + 1], rb_ref[base + c, h], val)
        return val

    o_ref[...] = jnp.where(rel > 0, side(half), side(0)) * LOG2E


def rel_bias_tiles(rel_bias, tq, tk):
    assert tq >= 128 and tk % tq == 0
    nt = tk // tq + 4
    return pl.pallas_call(
        functools.partial(_bias_kernel, tq=tq, tk=tk),
        grid=(DB_HEADS, nt),
        in_specs=[pl.BlockSpec(memory_space=pltpu.SMEM)],
        out_specs=pl.BlockSpec((None, None, tq, tk), lambda h, d: (h, d, 0, 0)),
        out_shape=jax.ShapeDtypeStruct((DB_HEADS, nt, tq, tk), jnp.float32),
        compiler_params=_cparams(2),
        name="rel_bias_tiles",
    )(rel_bias)


def _attn_kernel(q_ref, k_ref, v_ref, bias_ref, lp_ref, sw_ref, o_ref, m_ref, l_ref, acc_ref, *, lambda_init):
    kb = pl.program_id(3)
    tq = q_ref.shape[0]
    tk = k_ref.shape[0]

    @pl.when(kb == 0)
    def _():
        m_ref[...] = jnp.full_like(m_ref, -jnp.inf)
        l_ref[...] = jnp.zeros_like(l_ref)
        acc_ref[...] = jnp.zeros_like(acc_ref)

    bias = bias_ref[...]
    ps, alphas = [], []
    for c in range(2):
        sl = slice(c * DB_HEAD, (c + 1) * DB_HEAD)
        s = lax.dot_general(q_ref[:, sl], k_ref[:, sl], _NT, preferred_element_type=jnp.float32) + bias
        m_prev = m_ref[c]
        m_new = jnp.maximum(m_prev, jnp.max(s, axis=-1, keepdims=True))
        alpha = jnp.exp2(m_prev - m_new)
        p = jnp.exp2(s - jnp.tile(m_new, (1, tk // LANES)))
        l_ref[c] = alpha * l_ref[c] + jnp.sum(p, axis=-1, keepdims=True)
        m_ref[c] = m_new
        ps.append(p.astype(jnp.bfloat16))
        alphas.append(alpha)
    pv = jnp.dot(jnp.concatenate(ps, axis=0), v_ref[...], preferred_element_type=jnp.float32)
    acc_ref[...] = jnp.tile(jnp.concatenate(alphas, axis=0), (1, 2)) * acc_ref[...] + pv

    @pl.when(kb == pl.num_programs(3) - 1)
    def _():
        lp = lp_ref[...]
        lam = (jnp.exp(jnp.sum(lp[0:1, :] * lp[1:2, :], axis=-1, keepdims=True))
               - jnp.exp(jnp.sum(lp[2:3, :] * lp[3:4, :], axis=-1, keepdims=True)) + lambda_init)
        o = (acc_ref[0:tq, :] / jnp.tile(l_ref[0], (1, 2))
             - lam * (acc_ref[tq:2 * tq, :] / jnp.tile(l_ref[1], (1, 2))))
        ms = jnp.mean(o * o, axis=-1, keepdims=True)
        o_ref[...] = (o * lax.rsqrt(ms + EPS) * sw_ref[...] * (1.0 - lambda_init)).astype(o_ref.dtype)


def diff_attention(qn, kn, vb, bias_tiles, lam_params, subln_w, lambda_init):
    B, H, S, W = qn.shape
    _, nt, tq, tk = bias_tiles.shape
    r = tk // tq

    def bias_idx(b, h, i, j):
        return (h, jnp.clip(r * j - i, -(r + 1), 2) + r + 1, 0, 0)

    return pl.pallas_call(
        functools.partial(_attn_kernel, lambda_init=lambda_init),
        grid=(B, H, S // tq, S // tk),
        in_specs=[
            pl.BlockSpec((None, None, tq, W), lambda b, h, i, j: (b, h, i, 0)),
            pl.BlockSpec((None, None, tk, W), lambda b, h, i, j: (b, h, j, 0)),
            pl.BlockSpec((None, None, tk, W), lambda b, h, i, j: (b, h, j, 0)),
            pl.BlockSpec((None, None, tq, tk), bias_idx),
            pl.BlockSpec((4, DB_HEAD), lambda b, h, i, j: (0, 0)),
            pl.BlockSpec((1, W), lambda b, h, i, j: (0, 0)),
        ],
        out_specs=pl.BlockSpec((None, tq, W), lambda b, h, i, j: (b, i, h)),
        out_shape=jax.ShapeDtypeStruct((B, S, DB_WIDTH), jnp.bfloat16),
        scratch_shapes=[pltpu.VMEM((2, tq, LANES), jnp.float32), pltpu.VMEM((2, tq, LANES), jnp.float32),
                        pltpu.VMEM((2 * tq, W), jnp.float32)],
        compiler_params=_cparams(4),
        name="diff_attention",
    )(qn, kn, vb, bias_tiles, lam_params, subln_w.reshape(1, W))


ATTN_TQ = 512
ATTN_TK = 1024


def _trunk(x, mod, layers, rel_bias):
    S = x.shape[1]
    bias_tiles = rel_bias_tiles(rel_bias, min(ATTN_TQ, S), min(ATTN_TK, S))
    for l, p in enumerate(layers):
        m = mod[l]
        lambda_init = 0.8 - 0.6 * math.exp(-0.3 * l)
        h = norm_modulate(x, p["norm_attn_w"], m, 0, 1)
        zg = matmul_grouped(h, p["w_in"], jnp.float32, tm=1024, tn=1024)
        o_fwd = hgrn_pass(zg, p["gates_fwd"], reverse=False)
        out_a = hgrn_pass(zg, p["gates_bwd"], reverse=True, o_fwd=o_fwd, norm_w=p["hgrn_norm_w"])
        qn, kn, vb = qk_norm(zg, p["q_norm_w"], p["k_norm_w"])
        out_b = diff_attention(qn, kn, vb, bias_tiles, p["diff_lambda"], p["diff_subln_w"], lambda_init)
        x = matmul_residual([out_a, out_b], p["w_out"], x, m, 2, tm=1024, tn=1024)
        h = norm_modulate(x, p["norm_ffn_w"], m, 3, 4)
        ug = matmul_grouped(h, p["w_up"], jnp.float32, tm=1024, tn=512)
        act = conv_gate(ug, p["conv_w"], p["conv_b"])
        x = matmul_residual([act], p["w_down"], x, m, 5, tm=512, tn=512)
    return x


def kernel(x_prompt, x_sample, c_prompt, c_sample, w_ada, b_ada, norm_attn_w, w_in, hgrn_lb, hgrn_norm_w,
           q_norm_w, k_norm_w, diff_lambda, diff_subln_w, rel_bias, w_out, norm_ffn_w, w_up, conv_w, conv_b,
           w_down):
    L = w_ada.shape[0]
    D = x_prompt.shape[-1]
    Bp, Bs = c_prompt.shape[0], c_sample.shape[0]
    rows = -(-(Bp + Bs) // 8) * 8
    c_all = jnp.zeros((rows, D), jnp.float32).at[:Bp].set(c_prompt).at[Bp:Bp + Bs].set(c_sample)
    mod = adaln_mod(c_all, w_ada, b_ada)
    mod_p = mod[:, :Bp].reshape(L, Bp, 6, 1, D)
    mod_s = mod[:, Bp:Bp + Bs].reshape(L, Bs, 6, 1, D)

    lb_cum = jnp.cumsum(jax.nn.softmax(hgrn_lb.astype(jnp.float32), axis=0), axis=0)
    lb_all = lb_cum - lb_cum[0:1]
    log_lb, log_1m_lb, one_m_lb = jnp.log(lb_all), jnp.log1p(-lb_all), 1.0 - lb_all

    layers = []
    for l in range(L):
        def gates(d):
            return (log_lb[l, d][None], log_1m_lb[l, d][None], one_m_lb[l, d][None])
        layers.append(dict(
            norm_attn_w=norm_attn_w[l], w_in=w_in[l].astype(jnp.bfloat16),
            gates_fwd=gates(0), gates_bwd=gates(1), hgrn_norm_w=hgrn_norm_w[l],
            q_norm_w=q_norm_w[l], k_norm_w=k_norm_w[l], diff_lambda=diff_lambda[l].astype(jnp.float32),
            diff_subln_w=diff_subln_w[l], w_out=w_out[l].astype(jnp.bfloat16), norm_ffn_w=norm_ffn_w[l],
            w_up=w_up[l].astype(jnp.bfloat16), conv_w=conv_w[l], conv_b=conv_b[l],
            w_down=w_down[l].astype(jnp.bfloat16)))
    rel_bias = rel_bias.astype(jnp.float32)
    y_prompt = _trunk(x_prompt, mod_p, layers, rel_bias)
    y_sample = _trunk(x_sample, mod_s, layers, rel_bias)
    return (y_prompt, y_sample)
```

```python
import functools
import math

import jax
import jax.numpy as jnp
from jax import lax
from jax.experimental import pallas as pl
from jax.experimental.pallas import tpu as pltpu

D_MODEL = 4096
HA_HEADS = 16
HA_HEAD = 128
HA_WIDTH = HA_HEADS * HA_HEAD
DB_HEADS = 8
DB_HEAD = 128
DB_WIDTH = DB_HEADS * 2 * DB_HEAD
IN_COLS = 5 * HA_WIDTH + 3 * DB_WIDTH
CHUNK = 64
NUM_BUCKETS = 32
EPS = 1e-6
BUCKET_START = (0, 1, 2, 3, 4, 5, 6, 7, 8, 12, 16, 23, 32, 46, 64, 91)

LANES = 128
V7X_VMEM_BYTES = 64 * 1024 * 1024
VMEM_LIMIT = V7X_VMEM_BYTES - 8 * 1024 * 1024

LOG2E = 1.0 / math.log(2.0)
SAFE_SPAN = 80.0

_NT = (((1,), (1,)), ((), ()))
_TN = (((0,), (0,)), ((), ()))


def _cparams(n_axes):
    return pltpu.CompilerParams(
        dimension_semantics=("arbitrary",) * n_axes, vmem_limit_bytes=VMEM_LIMIT
    )


def _silu(x):
    return x * jax.nn.sigmoid(x)


def _ada_kernel(c_ref, w_ref, b_ref, o_ref):
    @pl.when(pl.program_id(2) == 0)
    def _():
        o_ref[...] = jnp.broadcast_to(b_ref[...], o_ref.shape)

    a = _silu(c_ref[...]).astype(jnp.bfloat16)
    o_ref[...] += jnp.dot(a, w_ref[...].astype(jnp.bfloat16), preferred_element_type=jnp.float32)


def adaln_mod(c_all, w_ada, b_ada, *, tk=1024, tn=2048):
    R, D = c_all.shape
    L, _, N = w_ada.shape
    return pl.pallas_call(
        _ada_kernel,
        grid=(L, N // tn, D // tk),
        in_specs=[
            pl.BlockSpec((R, tk), lambda l, j, k: (0, k)),
            pl.BlockSpec((None, tk, tn), lambda l, j, k: (l, k, j)),
            pl.BlockSpec((None, 1, tn), lambda l, j, k: (l, 0, j)),
        ],
        out_specs=pl.BlockSpec((None, R, tn), lambda l, j, k: (l, 0, j)),
        out_shape=jax.ShapeDtypeStruct((L, R, N), jnp.float32),
        compiler_params=_cparams(3),
        name="adaln_mod",
    )(c_all, w_ada, b_ada.reshape(L, 1, N))


def _normmod_kernel(x_ref, w_ref, sc_ref, sh_ref, o_ref):
    x = x_ref[...]
    ms = jnp.mean(x * x, axis=-1, keepdims=True)
    y = x * lax.rsqrt(ms + EPS) * w_ref[...]
    o_ref[...] = (y * (1.0 + sc_ref[...]) + sh_ref[...]).astype(o_ref.dtype)


def norm_modulate(x, w, mod, shift_idx, scale_idx, *, ts=512):
    B, S, D = x.shape
    ts = min(ts, S)
    return pl.pallas_call(
        _normmod_kernel,
        grid=(B, S // ts),
        in_specs=[
            pl.BlockSpec((None, ts, D), lambda b, i: (b, i, 0)),
            pl.BlockSpec((1, D), lambda b, i: (0, 0)),
            pl.BlockSpec((None, None, 1, D), lambda b, i: (b, scale_idx, 0, 0)),
            pl.BlockSpec((None, None, 1, D), lambda b, i: (b, shift_idx, 0, 0)),
        ],
        out_specs=pl.BlockSpec((None, ts, D), lambda b, i: (b, i, 0)),
        out_shape=jax.ShapeDtypeStruct((B, S, D), jnp.bfloat16),
        compiler_params=_cparams(2),
        name="norm_modulate",
    )(x, w.reshape(1, D), mod, mod)


def _mm_grouped_kernel(a_ref, w_ref, o_ref):
    acc = jnp.dot(a_ref[...], w_ref[...], preferred_element_type=jnp.float32)
    for g in range(o_ref.shape[0]):
        o_ref[g] = acc[:, g * LANES:(g + 1) * LANES].astype(o_ref.dtype)


def matmul_grouped(a, w, out_dtype, *, tm, tn):
    B, S, K = a.shape
    N = w.shape[1]
    tm, tn = min(tm, S), min(tn, N)
    return pl.pallas_call(
        _mm_grouped_kernel,
        grid=(B, S // tm, N // tn),
        in_specs=[
            pl.BlockSpec((None, tm, K), lambda b, i, j: (b, i, 0)),
            pl.BlockSpec((K, tn), lambda b, i, j: (0, j)),
        ],
        out_specs=pl.BlockSpec((None, tn // LANES, tm, LANES), lambda b, i, j: (b, j, i, 0)),
        out_shape=jax.ShapeDtypeStruct((B, N // LANES, S, LANES), out_dtype),
        compiler_params=_cparams(3),
        name="matmul_grouped",
    )(a, w)


def _mm_res_kernel(*refs, n_lhs):
    a_refs = refs[:n_lhs]
    w_ref, x_ref, g_ref, o_ref = refs[n_lhs:]
    acc = None
    k0 = 0
    for a_ref in a_refs:
        kw = a_ref.shape[-1]
        part = jnp.dot(a_ref[...], w_ref[k0:k0 + kw, :], preferred_element_type=jnp.float32)
        acc = part if acc is None else acc + part
        k0 += kw
    o_ref[...] = x_ref[...] + g_ref[...] * acc


def matmul_residual(lhs_parts, w, x, mod, gate_idx, *, tm, tn):
    B, S, N = x.shape
    K = w.shape[0]
    tm, tn = min(tm, S), min(tn, N)
    n_lhs = len(lhs_parts)
    in_specs = [pl.BlockSpec((None, tm, a.shape[-1]), lambda b, i, j: (b, i, 0)) for a in lhs_parts]
    in_specs += [
        pl.BlockSpec((K, tn), lambda b, i, j: (0, j)),
        pl.BlockSpec((None, tm, tn), lambda b, i, j: (b, i, j)),
        pl.BlockSpec((None, None, 1, tn), lambda b, i, j: (b, gate_idx, 0, j)),
    ]
    return pl.pallas_call(
        functools.partial(_mm_res_kernel, n_lhs=n_lhs),
        grid=(B, S // tm, N // tn),
        in_specs=in_specs,
        out_specs=pl.BlockSpec((None, tm, tn), lambda b, i, j: (b, i, j)),
        out_shape=jax.ShapeDtypeStruct((B, S, N), jnp.float32),
        compiler_params=_cparams(3),
        name="matmul_residual",
    )(*lhs_parts, w, x, mod)


def _convgate_kernel(uv_ref, uvp_ref, uvn_ref, ug_ref, ugp_ref, ugn_ref, wv_ref, wg_ref, bv_ref, bg_ref, o_ref):
    i = pl.program_id(1)
    last = pl.num_programs(1) - 1
    gc, ts, _ = uv_ref.shape
    rows = lax.broadcasted_iota(jnp.int32, (ts, LANES), 0)

    def conv(g, cur_ref, prev_ref, next_ref, w_ref, b_ref):
        sl = slice(g * LANES, (g + 1) * LANES)
        cur = cur_ref[g]
        prow = jnp.where(i > 0, prev_ref[g][7:8, :], 0.0)
        nrow = jnp.where(i < last, next_ref[g][0:1, :], 0.0)
        up = jnp.where(rows == 0, prow, pltpu.roll(cur, 1, 0))
        dn = jnp.where(rows == ts - 1, nrow, pltpu.roll(cur, ts - 1, 0))
        return up * w_ref[0:1, sl] + cur * w_ref[1:2, sl] + dn * w_ref[2:3, sl] + b_ref[:, sl]

    for g in range(gc):
        val = conv(g, uv_ref, uvp_ref, uvn_ref, wv_ref, bv_ref)
        gate = conv(g, ug_ref, ugp_ref, ugn_ref, wg_ref, bg_ref)
        o_ref[:, g * LANES:(g + 1) * LANES] = (_silu(gate) * val).astype(o_ref.dtype)


def conv_gate(u, conv_w, conv_b, *, ts=1024, gc=2):
    B, G2, S, _ = u.shape
    nf = G2 // 2
    F = nf * LANES
    ts = min(ts, S)
    nj = nf // gc
    r8 = ts // 8

    def cur(off):
        return pl.BlockSpec((None, gc, ts, LANES), lambda b, i, j: (b, off + j, i, 0))

    def prev(off):
        return pl.BlockSpec((None, gc, 8, LANES), lambda b, i, j: (b, off + j, jnp.maximum(i * r8 - 1, 0), 0))

    def nxt(off):
        return pl.BlockSpec((None, gc, 8, LANES),
                            lambda b, i, j: (b, off + j, jnp.minimum((i + 1) * r8, S // 8 - 1), 0))

    def par(rows, off):
        return pl.BlockSpec((rows, gc * LANES), lambda b, i, j: (0, off + j))

    cb = conv_b.reshape(1, 2 * F)
    return pl.pallas_call(
        _convgate_kernel,
        grid=(B, S // ts, nj),
        in_specs=[cur(0), prev(0), nxt(0), cur(nj), prev(nj), nxt(nj), par(3, 0), par(3, nj), par(1, 0), par(1, nj)],
        out_specs=pl.BlockSpec((None, ts, gc * LANES), lambda b, i, j: (b, i, j)),
        out_shape=jax.ShapeDtypeStruct((B, S, F), jnp.bfloat16),
        compiler_params=_cparams(3),
        name="conv_gate",
    )(u, u, u, u, u, u, conv_w, conv_w, cb, cb)


def _hgrn_kernel(*refs, reverse, T):
    if reverse:
        (xq_ref, xf_ref, xi_ref, xg_ref, of_ref, llb_ref, l1m_ref, oml_ref, nw_ref, o_ref,
         st_ref, oi_ref, q_ref, k_ref, b_ref) = refs
    else:
        xq_ref, xf_ref, xi_ref, llb_ref, l1m_ref, oml_ref, o_ref, st_ref, oi_ref, q_ref, k_ref, b_ref = refs
    C = CHUNK
    nC = T // C
    half = C // 2
    quarter = C // 4

    @pl.when(pl.program_id(2) == 0)
    def _():
        st_ref[...] = jnp.zeros_like(st_ref)

    xf = xf_ref[...]
    llb, l1m, oml = llb_ref[...], l1m_ref[...], oml_ref[...]
    q = _silu(xq_ref[...])
    v16 = xi_ref[...].astype(jnp.bfloat16)
    e = jnp.exp(-jnp.abs(xf))
    y = l1m + jnp.minimum(xf, 0.0) - jnp.log1p(e)
    logf = jnp.maximum(llb, y) + jnp.log1p(jnp.exp(-jnp.abs(llb - y)))
    kk = oml * jnp.where(xf >= 0.0, e, 1.0) / (1.0 + e)
    span = jnp.sum((-logf).reshape(T // quarter, quarter, LANES), axis=1)
    exact_needed = jnp.max(span) > SAFE_SPAN

    rc = lax.broadcasted_iota(jnp.int32, (T, LANES), 0) & (C - 1)
    b = logf
    for sft in (1, 2, 4, 8, 16, 32):
        if reverse:
            b = b + jnp.where(rc < C - sft, pltpu.roll(b, T - sft, 0), 0.0)
        else:
            b = b + jnp.where(rc >= sft, pltpu.roll(b, sft, 0), 0.0)

    def rows(x, c):
        return x[c * C:(c + 1) * C, :]

    def b_last(bc):
        return bc[0:1, :] if reverse else bc[C - 1:C, :]

    ti = lax.broadcasted_iota(jnp.int32, (C, C), 0)
    si = lax.broadcasted_iota(jnp.int32, (C, C), 1)
    causal = (si >= ti) if reverse else (si <= ti)

    @pl.when(jnp.logical_not(exact_needed))
    def _():
        first = lax.broadcasted_iota(jnp.int32, (C, LANES), 0) < half
        m0, m1, bnd = (quarter, half + quarter, half) if reverse else (quarter - 1, half + quarter - 1, half - 1)
        zero = jnp.zeros((C, LANES), jnp.bfloat16)
        for c in range(nC):
            bc = rows(b, c)
            mid = jnp.where(first, bc[m0:m0 + 1, :], bc[m1:m1 + 1, :])
            qa = (rows(q, c) * jnp.exp(bc - mid)).astype(jnp.bfloat16)
            ka = (rows(kk, c) * jnp.exp(mid - bc)).astype(jnp.bfloat16)
            eb = jnp.exp(-jnp.abs(bc - bc[bnd:bnd + 1, :]))
            qb = (rows(q, c) * eb).astype(jnp.bfloat16)
            kb = (rows(kk, c) * eb).astype(jnp.bfloat16)
            q_side = first if reverse else jnp.logical_not(first)
            qcat = jnp.concatenate([jnp.where(first, qa, zero), jnp.where(first, zero, qa),
                                    jnp.where(q_side, qb, zero)], axis=1)
            kcat = jnp.concatenate([jnp.where(first, ka, zero), jnp.where(first, zero, ka),
                                    jnp.where(q_side, zero, kb)], axis=1)
            s = lax.dot_general(qcat, kcat, _NT, preferred_element_type=jnp.float32)
            s = jnp.where(causal, s, 0.0).astype(jnp.bfloat16)
            oi_ref[c * C:(c + 1) * C, :] = jnp.dot(s, rows(v16, c), preferred_element_type=jnp.float32)

    @pl.when(exact_needed)
    def _():
        q_ref[...] = q
        k_ref[...] = kk
        b_ref[...] = b
        row = lax.broadcasted_iota(jnp.int32, (C, LANES), 0)

        def per_chunk(c, carry):
            r0 = pl.multiple_of(c * C, C)
            qc = q_ref[pl.ds(r0, C), :]
            kc = k_ref[pl.ds(r0, C), :]
            bc = b_ref[pl.ds(r0, C), :]
            vc = xi_ref[pl.ds(r0, C), :]

            def pair(d, acc):
                sh = ((C - d) % C) if reverse else d
                valid = (row < C - d) if reverse else (row >= d)
                dec = jnp.exp(jnp.where(valid, bc - pltpu.roll(bc, sh, 0), -jnp.inf))
                w = jnp.sum(qc * pltpu.roll(kc, sh, 0) * dec, axis=-1, keepdims=True)
                return acc + w * pltpu.roll(vc, sh, 0)

            oi_ref[pl.ds(r0, C), :] = lax.fori_loop(0, C, pair, jnp.zeros((C, LANES), jnp.float32))
            return carry

        lax.fori_loop(0, nC, per_chunk, 0)

    st = st_ref[...]
    for c in (range(nC - 1, -1, -1) if reverse else range(nC)):
        bc = rows(b, c)
        bl = b_last(bc)
        qa = (rows(q, c) * jnp.exp(bc)).astype(jnp.bfloat16)
        o = oi_ref[c * C:(c + 1) * C, :] + lax.dot_general(qa, st.astype(jnp.bfloat16), _NT,
                                                            preferred_element_type=jnp.float32)
        kl = (rows(kk, c) * jnp.exp(bl - bc)).astype(jnp.bfloat16)
        st = st * jnp.exp(bl) + lax.dot_general(rows(v16, c), kl, _TN, preferred_element_type=jnp.float32)
        if reverse:
            tot = of_ref[c * C:(c + 1) * C, :] + o
            ms = jnp.mean(tot * tot, axis=-1, keepdims=True)
            yn = tot * lax.rsqrt(ms + EPS) * nw_ref[...]
            o_ref[c * C:(c + 1) * C, :] = (yn * _silu(xg_ref[c * C:(c + 1) * C, :])).astype(o_ref.dtype)
        else:
            o_ref[c * C:(c + 1) * C, :] = o
    st_ref[...] = st


def hgrn_pass(zg, gate_params, *, reverse, o_fwd=None, norm_w=None, T=512):
    B, _, S, _ = zg.shape
    T = min(T, S)
    nblk = S // T
    H = HA_HEADS

    def seq(c):
        return (nblk - 1 - c) if reverse else c

    def zspec(section):
        return pl.BlockSpec((None, None, T, LANES), lambda b, h, c: (b, section * H + h, seq(c), 0))

    hspec = pl.BlockSpec((None, None, T, LANES), lambda b, h, c: (b, h, seq(c), 0))
    pspec = pl.BlockSpec((1, LANES), lambda b, h, c: (0, h))
    llb, l1m, oml = gate_params
    if reverse:
        in_specs = [zspec(0), zspec(2), zspec(3), zspec(4), hspec,
                    pspec, pspec, pspec, pl.BlockSpec((1, LANES), lambda b, h, c: (0, 0))]
        args = (zg, zg, zg, zg, o_fwd, llb, l1m, oml, norm_w.reshape(1, LANES))
        out_spec = pl.BlockSpec((None, T, LANES), lambda b, h, c: (b, seq(c), h))
        out_shape = jax.ShapeDtypeStruct((B, S, HA_WIDTH), jnp.bfloat16)
    else:
        in_specs = [zspec(0), zspec(1), zspec(3), pspec, pspec, pspec]
        args = (zg, zg, zg, llb, l1m, oml)
        out_spec = hspec
        out_shape = jax.ShapeDtypeStruct((B, H, S, LANES), jnp.float32)
    blk = pltpu.VMEM((T, LANES), jnp.float32)
    return pl.pallas_call(
        functools.partial(_hgrn_kernel, reverse=reverse, T=T),
        grid=(B, H, nblk),
        in_specs=in_specs,
        out_specs=out_spec,
        out_shape=out_shape,
        scratch_shapes=[pltpu.VMEM((HA_HEAD, HA_HEAD), jnp.float32), blk, blk, blk, blk],
        compiler_params=_cparams(3),
        name="hgrn_bwd" if reverse else "hgrn_fwd",
    )(*args)


def _qknorm_kernel(q_ref, k_ref, v_ref, qw_ref, kw_ref, qo_ref, ko_ref, vo_ref, *, q_scale):
    qw = qw_ref[...] * q_scale
    kw = kw_ref[...]
    for g in range(q_ref.shape[0]):
        h, c = divmod(g, 2)
        sl = slice(c * DB_HEAD, (c + 1) * DB_HEAD)
        for x_ref, w, o_ref in ((q_ref, qw, qo_ref), (k_ref, kw, ko_ref)):
            x = x_ref[g]
            ms = jnp.mean(x * x, axis=-1, keepdims=True)
            o_ref[h, :, sl] = (x * lax.rsqrt(ms + EPS) * w).astype(o_ref.dtype)
        vo_ref[h, :, sl] = v_ref[g].astype(vo_ref.dtype)


def qk_norm(zg, q_norm_w, k_norm_w, *, ts=512):
    B, _, S, _ = zg.shape
    ts = min(ts, S)
    ng = DB_WIDTH // LANES
    first = 5 * HA_WIDTH // DB_WIDTH

    def zspec(sec):
        return pl.BlockSpec((None, ng, ts, LANES), lambda b, i: (b, first + sec, i, 0))

    wspec = pl.BlockSpec((1, DB_HEAD), lambda b, i: (0, 0))
    ospec = pl.BlockSpec((None, DB_HEADS, ts, 2 * DB_HEAD), lambda b, i: (b, 0, i, 0))
    oshape = jax.ShapeDtypeStruct((B, DB_HEADS, S, 2 * DB_HEAD), jnp.bfloat16)
    return pl.pallas_call(
        functools.partial(_qknorm_kernel, q_scale=DB_HEAD ** -0.5 * LOG2E),
        grid=(B, S // ts),
        in_specs=[zspec(0), zspec(1), zspec(2), wspec, wspec],
        out_specs=(ospec, ospec, ospec),
        out_shape=(oshape, oshape, oshape),
        compiler_params=_cparams(2),
        name="qk_norm",
    )(zg, zg, zg, q_norm_w.reshape(1, DB_HEAD), k_norm_w.reshape(1, DB_HEAD))


def _bias_kernel(rb_ref, o_ref, *, tq, tk):
    h = pl.program_id(0)
    offset = (pl.program_id(1) - (tk // tq + 1)) * tq
    qi = lax.broadcasted_iota(jnp.int32, (tq, tk), 0)
    kj = lax.broadcasted_iota(jnp.int32, (tq, tk), 1)
    rel = kj - qi + offset
    n = jnp.abs(rel)
    half = NUM_BUCKETS // 2

    def side(base):
        val = jnp.full((tq, tk), rb_ref[base + half - 1, h], jnp.float32)
        for c in range(half - 2, -1, -1):
            val = jnp.where(n < BUCKET_START[c + 1], rb_ref[base + c, h], val)
        return val

    o_ref[...] = jnp.where(rel > 0, side(half), side(0)) * LOG2E


def rel_bias_tiles(rel_bias, tq, tk):
    assert tq >= 128 and tk % tq == 0
    nt = tk // tq + 4
    return pl.pallas_call(
        functools.partial(_bias_kernel, tq=tq, tk=tk),
        grid=(DB_HEADS, nt),
        in_specs=[pl.BlockSpec(memory_space=pltpu.SMEM)],
        out_specs=pl.BlockSpec((None, None, tq, tk), lambda h, d: (h, d, 0, 0)),
        out_shape=jax.ShapeDtypeStruct((DB_HEADS, nt, tq, tk), jnp.float32),
        compiler_params=_cparams(2),
        name="rel_bias_tiles",
    )(rel_bias)


def _attn_kernel(q_ref, k_ref, v_ref, bias_ref, lp_ref, sw_ref, o_ref, m_ref, l_ref, acc_ref, *, lambda_init):
    kb = pl.program_id(3)
    tq = q_ref.shape[0]
    tk = k_ref.shape[0]

    @pl.when(kb == 0)
    def _():
        m_ref[...] = jnp.full_like(m_ref, -jnp.inf)
        l_ref[...] = jnp.zeros_like(l_ref)
        acc_ref[...] = jnp.zeros_like(acc_ref)

    bias = bias_ref[...]
    ps, alphas = [], []
    for c in range(2):
        sl = slice(c * DB_HEAD, (c + 1) * DB_HEAD)
        s = lax.dot_general(q_ref[:, sl], k_ref[:, sl], _NT, preferred_element_type=jnp.float32) + bias
        m_prev = m_ref[c]
        m_new = jnp.maximum(m_prev, jnp.max(s, axis=-1, keepdims=True))
        alpha = jnp.exp2(m_prev - m_new)
        p = jnp.exp2(s - jnp.tile(m_new, (1, tk // LANES)))
        l_ref[c] = alpha * l_ref[c] + jnp.sum(p, axis=-1, keepdims=True)
        m_ref[c] = m_new
        ps.append(p.astype(jnp.bfloat16))
        alphas.append(alpha)
    pv = jnp.dot(jnp.concatenate(ps, axis=0), v_ref[...], preferred_element_type=jnp.float32)
    acc_ref[...] = jnp.tile(jnp.concatenate(alphas, axis=0), (1, 2)) * acc_ref[...] + pv

    @pl.when(kb == pl.num_programs(3) - 1)
    def _():
        lp = lp_ref[...]
        lam = (jnp.exp(jnp.sum(lp[0:1, :] * lp[1:2, :], axis=-1, keepdims=True))
               - jnp.exp(jnp.sum(lp[2:3, :] * lp[3:4, :], axis=-1, keepdims=True)) + lambda_init)
        o = (acc_ref[0:tq, :] / jnp.tile(l_ref[0], (1, 2))
             - lam * (acc_ref[tq:2 * tq, :] / jnp.tile(l_ref[1], (1, 2))))
        ms = jnp.mean(o * o, axis=-1, keepdims=True)
        o_ref[...] = (o * lax.rsqrt(ms + EPS) * sw_ref[...] * (1.0 - lambda_init)).astype(o_ref.dtype)


def diff_attention(qn, kn, vb, bias_tiles, lam_params, subln_w, lambda_init):
    B, H, S, W = qn.shape
    _, nt, tq, tk = bias_tiles.shape
    r = tk // tq

    def bias_idx(b, h, i, j):
        return (h, jnp.clip(r * j - i, -(r + 1), 2) + r + 1, 0, 0)

    return pl.pallas_call(
        functools.partial(_attn_kernel, lambda_init=lambda_init),
        grid=(B, H, S // tq, S // tk),
        in_specs=[
            pl.BlockSpec((None, None, tq, W), lambda b, h, i, j: (b, h, i, 0)),
            pl.BlockSpec((None, None, tk, W), lambda b, h, i, j: (b, h, j, 0)),
            pl.BlockSpec((None, None, tk, W), lambda b, h, i, j: (b, h, j, 0)),
            pl.BlockSpec((None, None, tq, tk), bias_idx),
            pl.BlockSpec((4, DB_HEAD), lambda b, h, i, j: (0, 0)),
            pl.BlockSpec((1, W), lambda b, h, i, j: (0, 0)),
        ],
        out_specs=pl.BlockSpec((None, tq, W), lambda b, h, i, j: (b, i, h)),
        out_shape=jax.ShapeDtypeStruct((B, S, DB_WIDTH), jnp.bfloat16),
        scratch_shapes=[pltpu.VMEM((2, tq, LANES), jnp.float32), pltpu.VMEM((2, tq, LANES), jnp.float32),
                        pltpu.VMEM((2 * tq, W), jnp.float32)],
        compiler_params=_cparams(4),
        name="diff_attention",
    )(qn, kn, vb, bias_tiles, lam_params, subln_w.reshape(1, W))


ATTN_TQ = 512
ATTN_TK = 1024


def _trunk(x, mod, layers, rel_bias):
    S = x.shape[1]
    bias_tiles = rel_bias_tiles(rel_bias, min(ATTN_TQ, S), min(ATTN_TK, S))
    for l, p in enumerate(layers):
        m = mod[l]
        lambda_init = 0.8 - 0.6 * math.exp(-0.3 * l)
        h = norm_modulate(x, p["norm_attn_w"], m, 0, 1)
        zg = matmul_grouped(h, p["w_in"], jnp.float32, tm=1024, tn=1024)
        o_fwd = hgrn_pass(zg, p["gates_fwd"], reverse=False)
        out_a = hgrn_pass(zg, p["gates_bwd"], reverse=True, o_fwd=o_fwd, norm_w=p["hgrn_norm_w"])
        qn, kn, vb = qk_norm(zg, p["q_norm_w"], p["k_norm_w"])
        out_b = diff_attention(qn, kn, vb, bias_tiles, p["diff_lambda"], p["diff_subln_w"], lambda_init)
        x = matmul_residual([out_a, out_b], p["w_out"], x, m, 2, tm=1024, tn=1024)
        h = norm_modulate(x, p["norm_ffn_w"], m, 3, 4)
        ug = matmul_grouped(h, p["w_up"], jnp.float32, tm=1024, tn=512)
        act = conv_gate(ug, p["conv_w"], p["conv_b"])
        x = matmul_residual([act], p["w_down"], x, m, 5, tm=512, tn=512)
    return x


def kernel(x_prompt, x_sample, c_prompt, c_sample, w_ada, b_ada, norm_attn_w, w_in, hgrn_lb, hgrn_norm_w,
           q_norm_w, k_norm_w, diff_lambda, diff_subln_w, rel_bias, w_out, norm_ffn_w, w_up, conv_w, conv_b,
           w_down):
    L = w_ada.shape[0]
    D = x_prompt.shape[-1]
    Bp, Bs = c_prompt.shape[0], c_sample.shape[0]
    rows = -(-(Bp + Bs) // 8) * 8
    c_all = jnp.zeros((rows, D), jnp.float32).at[:Bp].set(c_prompt).at[Bp:Bp + Bs].set(c_sample)
    mod = adaln_mod(c_all, w_ada, b_ada)
    mod_p = mod[:, :Bp].reshape(L, Bp, 6, 1, D)
    mod_s = mod[:, Bp:Bp + Bs].reshape(L, Bs, 6, 1, D)

    lb_cum = jnp.cumsum(jax.nn.softmax(hgrn_lb.astype(jnp.float32), axis=0), axis=0)
    lb_all = lb_cum - lb_cum[0:1]
    log_lb, log_1m_lb, one_m_lb = jnp.log(lb_all), jnp.log1p(-lb_all), 1.0 - lb_all

    layers = []
    for l in range(L):
        def gates(d):
            return (log_lb[l, d][None], log_1m_lb[l, d][None], one_m_lb[l, d][None])
        layers.append(dict(
            norm_attn_w=norm_attn_w[l], w_in=w_in[l].astype(jnp.bfloat16),
            gates_fwd=gates(0), gates_bwd=gates(1), hgrn_norm_w=hgrn_norm_w[l],
            q_norm_w=q_norm_w[l], k_norm_w=k_norm_w[l], diff_lambda=diff_lambda[l].astype(jnp.float32),
            diff_subln_w=diff_subln_w[l], w_out=w_out[l].astype(jnp.bfloat16), norm_ffn_w=norm_ffn_w[l],
            w_up=w_up[l].astype(jnp.bfloat16), conv_w=conv_w[l], conv_b=conv_b[l],
            w_down=w_down[l].astype(jnp.bfloat16)))
    rel_bias = rel_bias.astype(jnp.float32)
    y_prompt = _trunk(x_prompt, mod_p, layers, rel_bias)
    y_sample = _trunk(x_sample, mod_s, layers, rel_bias)
    return (y_prompt, y_sample)
```

```python
import functools
import math

import jax
import jax.numpy as jnp
from jax import lax
from jax.experimental import pallas as pl
from jax.experimental.pallas import tpu as pltpu

D_MODEL = 4096
HA_HEADS = 16
HA_HEAD = 128
HA_WIDTH = HA_HEADS * HA_HEAD
DB_HEADS = 8
DB_HEAD = 128
DB_WIDTH = DB_HEADS * 2 * DB_HEAD
IN_COLS = 5 * HA_WIDTH + 3 * DB_WIDTH
CHUNK = 64
NUM_BUCKETS = 32
EPS = 1e-6
BUCKET_START = (0, 1, 2, 3, 4, 5, 6, 7, 8, 12, 16, 23, 32, 46, 64, 91)

LANES = 128
V7X_VMEM_BYTES = 64 * 1024 * 1024
VMEM_LIMIT = V7X_VMEM_BYTES - 8 * 1024 * 1024

LOG2E = 1.0 / math.log(2.0)
SAFE_SPAN = 80.0

_NT = (((1,), (1,)), ((), ()))
_TN = (((0,), (0,)), ((), ()))


def _cparams(n_axes):
    return pltpu.CompilerParams(
        dimension_semantics=("arbitrary",) * n_axes, vmem_limit_bytes=VMEM_LIMIT
    )


def _silu(x):
    return x * jax.nn.sigmoid(x)


def _ada_kernel(c_ref, w_ref, b_ref, o_ref):
    @pl.when(pl.program_id(2) == 0)
    def _():
        o_ref[...] = jnp.broadcast_to(b_ref[...], o_ref.shape)

    a = _silu(c_ref[...]).astype(jnp.bfloat16)
    o_ref[...] += jnp.dot(a, w_ref[...].astype(jnp.bfloat16), preferred_element_type=jnp.float32)


def adaln_mod(c_all, w_ada, b_ada, *, tk=1024, tn=2048):
    R, D = c_all.shape
    L, _, N = w_ada.shape
    return pl.pallas_call(
        _ada_kernel,
        grid=(L, N // tn, D // tk),
        in_specs=[
            pl.BlockSpec((R, tk), lambda l, j, k: (0, k)),
            pl.BlockSpec((None, tk, tn), lambda l, j, k: (l, k, j)),
            pl.BlockSpec((None, 1, tn), lambda l, j, k: (l, 0, j)),
        ],
        out_specs=pl.BlockSpec((None, R, tn), lambda l, j, k: (l, 0, j)),
        out_shape=jax.ShapeDtypeStruct((L, R, N), jnp.float32),
        compiler_params=_cparams(3),
        name="adaln_mod",
    )(c_all, w_ada, b_ada.reshape(L, 1, N))


def _normmod_kernel(x_ref, w_ref, sc_ref, sh_ref, o_ref):
    x = x_ref[...]
    ms = jnp.mean(x * x, axis=-1, keepdims=True)
    y = x * lax.rsqrt(ms + EPS) * w_ref[...]
    o_ref[...] = (y * (1.0 + sc_ref[...]) + sh_ref[...]).astype(o_ref.dtype)


def norm_modulate(x, w, mod, shift_idx, scale_idx, *, ts=512):
    B, S, D = x.shape
    ts = min(ts, S)
    return pl.pallas_call(
        _normmod_kernel,
        grid=(B, S // ts),
        in_specs=[
            pl.BlockSpec((None, ts, D), lambda b, i: (b, i, 0)),
            pl.BlockSpec((1, D), lambda b, i: (0, 0)),
            pl.BlockSpec((None, None, 1, D), lambda b, i: (b, scale_idx, 0, 0)),
            pl.BlockSpec((None, None, 1, D), lambda b, i: (b, shift_idx, 0, 0)),
        ],
        out_specs=pl.BlockSpec((None, ts, D), lambda b, i: (b, i, 0)),
        out_shape=jax.ShapeDtypeStruct((B, S, D), jnp.bfloat16),
        compiler_params=_cparams(2),
        name="norm_modulate",
    )(x, w.reshape(1, D), mod, mod)


def _mm_grouped_kernel(a_ref, w_ref, o_ref):
    acc = jnp.dot(a_ref[...], w_ref[...], preferred_element_type=jnp.float32)
    for g in range(o_ref.shape[0]):
        o_ref[g] = acc[:, g * LANES:(g + 1) * LANES].astype(o_ref.dtype)


def matmul_grouped(a, w, layer, out_dtype, *, tm, tn):
    B, S, K = a.shape
    N = w.shape[2]
    tm, tn = min(tm, S), min(tn, N)
    return pl.pallas_call(
        _mm_grouped_kernel,
        grid=(B, S // tm, N // tn),
        in_specs=[
            pl.BlockSpec((None, tm, K), lambda b, i, j: (b, i, 0)),
            pl.BlockSpec((None, K, tn), lambda b, i, j: (layer, 0, j)),
        ],
        out_specs=pl.BlockSpec((None, tn // LANES, tm, LANES), lambda b, i, j: (b, j, i, 0)),
        out_shape=jax.ShapeDtypeStruct((B, N // LANES, S, LANES), out_dtype),
        compiler_params=_cparams(3),
        name="matmul_grouped",
    )(a, w)


def _mm_res_kernel(*refs, n_lhs):
    a_refs = refs[:n_lhs]
    w_ref, x_ref, g_ref, o_ref = refs[n_lhs:]
    acc = None
    k0 = 0
    for a_ref in a_refs:
        kw = a_ref.shape[-1]
        part = jnp.dot(a_ref[...], w_ref[k0:k0 + kw, :], preferred_element_type=jnp.float32)
        acc = part if acc is None else acc + part
        k0 += kw
    o_ref[...] = x_ref[...] + g_ref[...] * acc


def matmul_residual(lhs_parts, w, layer, x, mod, gate_idx, *, tm, tn):
    B, S, N = x.shape
    K = w.shape[1]
    tm, tn = min(tm, S), min(tn, N)
    n_lhs = len(lhs_parts)
    in_specs = [pl.BlockSpec((None, tm, a.shape[-1]), lambda b, i, j: (b, i, 0)) for a in lhs_parts]
    in_specs += [
        pl.BlockSpec((None, K, tn), lambda b, i, j: (layer, 0, j)),
        pl.BlockSpec((None, tm, tn), lambda b, i, j: (b, i, j)),
        pl.BlockSpec((None, None, 1, tn), lambda b, i, j: (b, gate_idx, 0, j)),
    ]
    return pl.pallas_call(
        functools.partial(_mm_res_kernel, n_lhs=n_lhs),
        grid=(B, S // tm, N // tn),
        in_specs=in_specs,
        out_specs=pl.BlockSpec((None, tm, tn), lambda b, i, j: (b, i, j)),
        out_shape=jax.ShapeDtypeStruct((B, S, N), jnp.float32),
        compiler_params=_cparams(3),
        name="matmul_residual",
    )(*lhs_parts, w, x, mod)


def _convgate_kernel(uv_ref, uvp_ref, uvn_ref, ug_ref, ugp_ref, ugn_ref, wv_ref, wg_ref, bv_ref, bg_ref, o_ref):
    i = pl.program_id(1)
    last = pl.num_programs(1) - 1
    gc, ts, _ = uv_ref.shape
    rows = lax.broadcasted_iota(jnp.int32, (ts, LANES), 0)

    def conv(g, cur_ref, prev_ref, next_ref, w_ref, b_ref):
        sl = slice(g * LANES, (g + 1) * LANES)
        cur = cur_ref[g]
        prow = jnp.where(i > 0, prev_ref[g][7:8, :], 0.0)
        nrow = jnp.where(i < last, next_ref[g][0:1, :], 0.0)
        up = jnp.where(rows == 0, prow, pltpu.roll(cur, 1, 0))
        dn = jnp.where(rows == ts - 1, nrow, pltpu.roll(cur, ts - 1, 0))
        return up * w_ref[0:1, sl] + cur * w_ref[1:2, sl] + dn * w_ref[2:3, sl] + b_ref[:, sl]

    for g in range(gc):
        val = conv(g, uv_ref, uvp_ref, uvn_ref, wv_ref, bv_ref)
        gate = conv(g, ug_ref, ugp_ref, ugn_ref, wg_ref, bg_ref)
        o_ref[:, g * LANES:(g + 1) * LANES] = (_silu(gate) * val).astype(o_ref.dtype)


def conv_gate(u, conv_w, conv_b, *, ts=1024, gc=2):
    B, G2, S, _ = u.shape
    nf = G2 // 2
    F = nf * LANES
    ts = min(ts, S)
    nj = nf // gc
    r8 = ts // 8

    def cur(off):
        return pl.BlockSpec((None, gc, ts, LANES), lambda b, i, j: (b, off + j, i, 0))

    def prev(off):
        return pl.BlockSpec((None, gc, 8, LANES), lambda b, i, j: (b, off + j, jnp.maximum(i * r8 - 1, 0), 0))

    def nxt(off):
        return pl.BlockSpec((None, gc, 8, LANES),
                            lambda b, i, j: (b, off + j, jnp.minimum((i + 1) * r8, S // 8 - 1), 0))

    def par(rows, off):
        return pl.BlockSpec((rows, gc * LANES), lambda b, i, j: (0, off + j))

    cb = conv_b.reshape(1, 2 * F)
    return pl.pallas_call(
        _convgate_kernel,
        grid=(B, S // ts, nj),
        in_specs=[cur(0), prev(0), nxt(0), cur(nj), prev(nj), nxt(nj), par(3, 0), par(3, nj), par(1, 0), par(1, nj)],
        out_specs=pl.BlockSpec((None, ts, gc * LANES), lambda b, i, j: (b, i, j)),
        out_shape=jax.ShapeDtypeStruct((B, S, F), jnp.bfloat16),
        compiler_params=_cparams(3),
        name="conv_gate",
    )(u, u, u, u, u, u, conv_w, conv_w, cb, cb)


def _hgrn_kernel(*refs, reverse, T):
    if reverse:
        (xq_ref, xf_ref, xi_ref, xg_ref, of_ref, llb_ref, l1m_ref, oml_ref, nw_ref, o_ref,
         st_ref, oin_ref, q_ref, k_ref, b_ref) = refs
    else:
        xq_ref, xf_ref, xi_ref, llb_ref, l1m_ref, oml_ref, o_ref, st_ref, oin_ref, q_ref, k_ref, b_ref = refs
    C = CHUNK
    nC = T // C
    half = C // 2
    quarter = C // 4

    @pl.when(pl.program_id(2) == 0)
    def _():
        st_ref[...] = jnp.zeros_like(st_ref)

    xf = xf_ref[...]
    llb, l1m, oml = llb_ref[...], l1m_ref[...], oml_ref[...]
    q = _silu(xq_ref[...])
    v16 = xi_ref[...].astype(jnp.bfloat16)
    e = jnp.exp(-jnp.abs(xf))
    y = l1m + jnp.minimum(xf, 0.0) - jnp.log(1.0 + e)
    logf = jnp.maximum(llb, y) + jnp.log(1.0 + jnp.exp(-jnp.abs(llb - y)))
    kk = oml * jnp.where(xf >= 0.0, e, 1.0) / (1.0 + e)
    span = jnp.sum((-logf).reshape(T // quarter, quarter, LANES), axis=1)
    exact_needed = jnp.max(span) > SAFE_SPAN

    rc = lax.broadcasted_iota(jnp.int32, (T, LANES), 0) & (C - 1)
    b = logf
    for sft in (1, 2, 4, 8, 16, 32):
        if reverse:
            b = b + jnp.where(rc < C - sft, pltpu.roll(b, T - sft, 0), 0.0)
        else:
            b = b + jnp.where(rc >= sft, pltpu.roll(b, sft, 0), 0.0)

    def rows(x, c):
        return x[c * C:(c + 1) * C, :]

    def b_last(bc):
        return bc[0:1, :] if reverse else bc[C - 1:C, :]

    ti = lax.broadcasted_iota(jnp.int32, (C, C), 0)
    si = lax.broadcasted_iota(jnp.int32, (C, C), 1)
    causal = (si >= ti) if reverse else (si <= ti)

    first = lax.broadcasted_iota(jnp.int32, (C, LANES), 0) < half
    q_side = first if reverse else jnp.logical_not(first)
    m0, m1, bnd = (quarter, half + quarter, half) if reverse else (quarter - 1, half + quarter - 1, half - 1)
    zero = jnp.zeros((C, LANES), jnp.bfloat16)

    def intra_factored(c, bc):
        mid = jnp.where(first, bc[m0:m0 + 1, :], bc[m1:m1 + 1, :])
        qa = (rows(q, c) * jnp.exp(bc - mid)).astype(jnp.bfloat16)
        ka = (rows(kk, c) * jnp.exp(mid - bc)).astype(jnp.bfloat16)
        eb = jnp.exp(-jnp.abs(bc - bc[bnd:bnd + 1, :]))
        qb = (rows(q, c) * eb).astype(jnp.bfloat16)
        kb = (rows(kk, c) * eb).astype(jnp.bfloat16)
        qcat = jnp.concatenate([jnp.where(first, qa, zero), jnp.where(first, zero, qa),
                                jnp.where(q_side, qb, zero)], axis=1)
        kcat = jnp.concatenate([jnp.where(first, ka, zero), jnp.where(first, zero, ka),
                                jnp.where(q_side, zero, kb)], axis=1)
        s = lax.dot_general(qcat, kcat, _NT, preferred_element_type=jnp.float32)
        s = jnp.where(causal, s, 0.0).astype(jnp.bfloat16)
        return jnp.dot(s, rows(v16, c), preferred_element_type=jnp.float32)

    def emit(rsl, o):
        if reverse:
            tot = of_ref[rsl, :] + o
            ms = jnp.mean(tot * tot, axis=-1, keepdims=True)
            yn = tot * lax.rsqrt(ms + EPS) * nw_ref[...]
            o_ref[rsl, :] = (yn * _silu(xg_ref[rsl, :])).astype(o_ref.dtype)
        else:
            o_ref[rsl, :] = o

    st = st_ref[...]
    for c in (range(nC - 1, -1, -1) if reverse else range(nC)):
        rsl = slice(c * C, (c + 1) * C)
        bc = rows(b, c)
        bl = b_last(bc)
        qa = (rows(q, c) * jnp.exp(bc)).astype(jnp.bfloat16)
        o_inter = lax.dot_general(qa, st.astype(jnp.bfloat16), _NT, preferred_element_type=jnp.float32)
        oin_ref[rsl, :] = o_inter
        emit(rsl, intra_factored(c, bc) + o_inter)
        kl = (rows(kk, c) * jnp.exp(bl - bc)).astype(jnp.bfloat16)
        st = st * jnp.exp(bl) + lax.dot_general(rows(v16, c), kl, _TN, preferred_element_type=jnp.float32)
    st_ref[...] = st

    @pl.when(exact_needed)
    def _():
        q_ref[...] = q
        k_ref[...] = kk
        b_ref[...] = b
        row = lax.broadcasted_iota(jnp.int32, (C, LANES), 0)

        def per_chunk(c, carry):
            rsl = pl.ds(pl.multiple_of(c * C, C), C)
            qc = q_ref[rsl, :]
            kc = k_ref[rsl, :]
            bc = b_ref[rsl, :]
            vc = xi_ref[rsl, :]

            def pair(d, acc):
                sh = ((C - d) % C) if reverse else d
                valid = (row < C - d) if reverse else (row >= d)
                dec = jnp.exp(jnp.where(valid, bc - pltpu.roll(bc, sh, 0), -jnp.inf))
                w = jnp.sum(qc * pltpu.roll(kc, sh, 0) * dec, axis=-1, keepdims=True)
                return acc + w * pltpu.roll(vc, sh, 0)

            oi = lax.fori_loop(0, C, pair, jnp.zeros((C, LANES), jnp.float32))
            emit(rsl, oi + oin_ref[rsl, :])
            return carry

        lax.fori_loop(0, nC, per_chunk, 0)


def hgrn_pass(zg, gate_params, *, reverse, o_fwd=None, norm_w=None, T=512):
    B, _, S, _ = zg.shape
    T = min(T, S)
    nblk = S // T
    H = HA_HEADS

    def seq(c):
        return (nblk - 1 - c) if reverse else c

    def zspec(section):
        return pl.BlockSpec((None, None, T, LANES), lambda b, h, c: (b, section * H + h, seq(c), 0))

    hspec = pl.BlockSpec((None, None, T, LANES), lambda b, h, c: (b, h, seq(c), 0))
    pspec = pl.BlockSpec((1, LANES), lambda b, h, c: (0, h))
    llb, l1m, oml = gate_params
    if reverse:
        in_specs = [zspec(0), zspec(2), zspec(3), zspec(4), hspec,
                    pspec, pspec, pspec, pl.BlockSpec((1, LANES), lambda b, h, c: (0, 0))]
        args = (zg, zg, zg, zg, o_fwd, llb, l1m, oml, norm_w.reshape(1, LANES))
        out_spec = pl.BlockSpec((None, T, LANES), lambda b, h, c: (b, seq(c), h))
        out_shape = jax.ShapeDtypeStruct((B, S, HA_WIDTH), jnp.bfloat16)
    else:
        in_specs = [zspec(0), zspec(1), zspec(3), pspec, pspec, pspec]
        args = (zg, zg, zg, llb, l1m, oml)
        out_spec = hspec
        out_shape = jax.ShapeDtypeStruct((B, H, S, LANES), jnp.float32)
    blk = pltpu.VMEM((T, LANES), jnp.float32)
    return pl.pallas_call(
        functools.partial(_hgrn_kernel, reverse=reverse, T=T),
        grid=(B, H, nblk),
        in_specs=in_specs,
        out_specs=out_spec,
        out_shape=out_shape,
        scratch_shapes=[pltpu.VMEM((HA_HEAD, HA_HEAD), jnp.float32), blk, blk, blk, blk],
        compiler_params=_cparams(3),
        name="hgrn_bwd" if reverse else "hgrn_fwd",
    )(*args)


def _qknorm_kernel(q_ref, k_ref, v_ref, qw_ref, kw_ref, qo_ref, ko_ref, vo_ref, *, q_scale):
    qw = qw_ref[...] * q_scale
    kw = kw_ref[...]
    for g in range(q_ref.shape[0]):
        h, c = divmod(g, 2)
        sl = slice(c * DB_HEAD, (c + 1) * DB_HEAD)
        for x_ref, w, o_ref in ((q_ref, qw, qo_ref), (k_ref, kw, ko_ref)):
            x = x_ref[g]
            ms = jnp.mean(x * x, axis=-1, keepdims=True)
            o_ref[h, :, sl] = (x * lax.rsqrt(ms + EPS) * w).astype(o_ref.dtype)
        vo_ref[h, :, sl] = v_ref[g].astype(vo_ref.dtype)


def qk_norm(zg, q_norm_w, k_norm_w, *, ts=512):
    B, _, S, _ = zg.shape
    ts = min(ts, S)
    ng = DB_WIDTH // LANES
    first = 5 * HA_WIDTH // DB_WIDTH

    def zspec(sec):
        return pl.BlockSpec((None, ng, ts, LANES), lambda b, i: (b, first + sec, i, 0))

    wspec = pl.BlockSpec((1, DB_HEAD), lambda b, i: (0, 0))
    ospec = pl.BlockSpec((None, DB_HEADS, ts, 2 * DB_HEAD), lambda b, i: (b, 0, i, 0))
    oshape = jax.ShapeDtypeStruct((B, DB_HEADS, S, 2 * DB_HEAD), jnp.bfloat16)
    return pl.pallas_call(
        functools.partial(_qknorm_kernel, q_scale=DB_HEAD ** -0.5 * LOG2E),
        grid=(B, S // ts),
        in_specs=[zspec(0), zspec(1), zspec(2), wspec, wspec],
        out_specs=(ospec, ospec, ospec),
        out_shape=(oshape, oshape, oshape),
        compiler_params=_cparams(2),
        name="qk_norm",
    )(zg, zg, zg, q_norm_w.reshape(1, DB_HEAD), k_norm_w.reshape(1, DB_HEAD))


def _bias_kernel(rb_ref, o_ref, *, tq, tk):
    h = pl.program_id(0)
    offset = (pl.program_id(1) - (tk // tq + 1)) * tq
    qi = lax.broadcasted_iota(jnp.int32, (tq, tk), 0)
    kj = lax.broadcasted_iota(jnp.int32, (tq, tk), 1)
    rel = kj - qi + offset
    n = jnp.abs(rel)
    half = NUM_BUCKETS // 2

    def side(base):
        val = jnp.full((tq, tk), rb_ref[base + half - 1, h], jnp.float32)
        for c in range(half - 2, -1, -1):
            val = jnp.where(n < BUCKET_START[c + 1], rb_ref[base + c, h], val)
        return val

    o_ref[...] = jnp.where(rel > 0, side(half), side(0)) * LOG2E


def rel_bias_tiles(rel_bias, tq, tk):
    assert tq >= 128 and tk % tq == 0
    nt = tk // tq + 4
    return pl.pallas_call(
        functools.partial(_bias_kernel, tq=tq, tk=tk),
        grid=(DB_HEADS, nt),
        in_specs=[pl.BlockSpec(memory_space=pltpu.SMEM)],
        out_specs=pl.BlockSpec((None, None, tq, tk), lambda h, d: (h, d, 0, 0)),
        out_shape=jax.ShapeDtypeStruct((DB_HEADS, nt, tq, tk), jnp.float32),
        compiler_params=_cparams(2),
        name="rel_bias_tiles",
    )(rel_bias)


def _attn_kernel(q_ref, k_ref, v_ref, bias_ref, lp_ref, sw_ref, o_ref, m_ref, l_ref, acc_ref, *, lambda_init):
    kb = pl.program_id(3)
    tk = k_ref.shape[0]
    n_sub = m_ref.shape[0] // 2
    ts = m_ref.shape[1]

    @pl.when(kb == 0)
    def _():
        m_ref[...] = jnp.full_like(m_ref, -jnp.inf)
        l_ref[...] = jnp.zeros_like(l_ref)
        acc_ref[...] = jnp.zeros_like(acc_ref)

    for sub in range(n_sub):
        rsl = slice(sub * ts, (sub + 1) * ts)
        bias = bias_ref[rsl, :]
        ps, alphas = [], []
        for c in range(2):
            g = 2 * sub + c
            sl = slice(c * DB_HEAD, (c + 1) * DB_HEAD)
            s = lax.dot_general(q_ref[rsl, sl], k_ref[:, sl], _NT, preferred_element_type=jnp.float32) + bias
            m_prev = m_ref[g]
            m_new = jnp.maximum(m_prev, jnp.max(s, axis=-1, keepdims=True))
            alpha = jnp.exp2(m_prev - m_new)
            p = jnp.exp2(s - jnp.tile(m_new, (1, tk // LANES)))
            l_ref[g] = alpha * l_ref[g] + jnp.sum(p, axis=-1, keepdims=True)
            m_ref[g] = m_new
            ps.append(p.astype(jnp.bfloat16))
            alphas.append(alpha)
        pv = jnp.dot(jnp.concatenate(ps, axis=0), v_ref[...], preferred_element_type=jnp.float32)
        asl = slice(2 * sub * ts, 2 * (sub + 1) * ts)
        acc_ref[asl, :] = jnp.tile(jnp.concatenate(alphas, axis=0), (1, 2)) * acc_ref[asl, :] + pv

    @pl.when(kb == pl.num_programs(3) - 1)
    def _():
        lp = lp_ref[...]
        lam = (jnp.exp(jnp.sum(lp[0:1, :] * lp[1:2, :], axis=-1, keepdims=True))
               - jnp.exp(jnp.sum(lp[2:3, :] * lp[3:4, :], axis=-1, keepdims=True)) + lambda_init)
        for sub in range(n_sub):
            a0 = acc_ref[2 * sub * ts:(2 * sub + 1) * ts, :]
            a1 = acc_ref[(2 * sub + 1) * ts:(2 * sub + 2) * ts, :]
            o = a0 / jnp.tile(l_ref[2 * sub], (1, 2)) - lam * (a1 / jnp.tile(l_ref[2 * sub + 1], (1, 2)))
            ms = jnp.mean(o * o, axis=-1, keepdims=True)
            o_ref[sub * ts:(sub + 1) * ts, :] = (o * lax.rsqrt(ms + EPS) * sw_ref[...]
                                                 * (1.0 - lambda_init)).astype(o_ref.dtype)


def diff_attention(qn, kn, vb, bias_tiles, lam_params, subln_w, lambda_init):
    B, H, S, W = qn.shape
    _, nt, tq, tk = bias_tiles.shape
    r = tk // tq
    ts = min(ATTN_SUB, tq)
    n_sub = tq // ts

    def bias_idx(b, h, i, j):
        return (h, jnp.clip(r * j - i, -(r + 1), 2) + r + 1, 0, 0)

    return pl.pallas_call(
        functools.partial(_attn_kernel, lambda_init=lambda_init),
        grid=(B, H, S // tq, S // tk),
        in_specs=[
            pl.BlockSpec((None, None, tq, W), lambda b, h, i, j: (b, h, i, 0)),
            pl.BlockSpec((None, None, tk, W), lambda b, h, i, j: (b, h, j, 0)),
            pl.BlockSpec((None, None, tk, W), lambda b, h, i, j: (b, h, j, 0)),
            pl.BlockSpec((None, None, tq, tk), bias_idx),
            pl.BlockSpec((4, DB_HEAD), lambda b, h, i, j: (0, 0)),
            pl.BlockSpec((1, W), lambda b, h, i, j: (0, 0)),
        ],
        out_specs=pl.BlockSpec((None, tq, W), lambda b, h, i, j: (b, i, h)),
        out_shape=jax.ShapeDtypeStruct((B, S, DB_WIDTH), jnp.bfloat16),
        scratch_shapes=[pltpu.VMEM((2 * n_sub, ts, LANES), jnp.float32),
                        pltpu.VMEM((2 * n_sub, ts, LANES), jnp.float32),
                        pltpu.VMEM((2 * tq, W), jnp.float32)],
        compiler_params=_cparams(4),
        name="diff_attention",
    )(qn, kn, vb, bias_tiles, lam_params, subln_w.reshape(1, W))


ATTN_TQ = 512
ATTN_SUB = 512
ATTN_TK = 1024


def _trunk(x, mod, layers, weights, bias_tiles):
    for l, p in enumerate(layers):
        m = mod[l]
        lambda_init = 0.8 - 0.6 * math.exp(-0.3 * l)
        h = norm_modulate(x, p["norm_attn_w"], m, 0, 1)
        zg = matmul_grouped(h, weights["w_in"], l, jnp.float32, tm=1024, tn=1024)
        o_fwd = hgrn_pass(zg, p["gates_fwd"], reverse=False)
        out_a = hgrn_pass(zg, p["gates_bwd"], reverse=True, o_fwd=o_fwd, norm_w=p["hgrn_norm_w"])
        qn, kn, vb = qk_norm(zg, p["q_norm_w"], p["k_norm_w"])
        out_b = diff_attention(qn, kn, vb, bias_tiles, p["diff_lambda"], p["diff_subln_w"], lambda_init)
        x = matmul_residual([out_a, out_b], weights["w_out"], l, x, m, 2, tm=1024, tn=1024)
        h = norm_modulate(x, p["norm_ffn_w"], m, 3, 4)
        ug = matmul_grouped(h, weights["w_up"], l, jnp.float32, tm=2048, tn=512)
        act = conv_gate(ug, p["conv_w"], p["conv_b"])
        x = matmul_residual([act], weights["w_down"], l, x, m, 5, tm=512, tn=512)
    return x


def kernel(x_prompt, x_sample, c_prompt, c_sample, w_ada, b_ada, norm_attn_w, w_in, hgrn_lb, hgrn_norm_w,
           q_norm_w, k_norm_w, diff_lambda, diff_subln_w, rel_bias, w_out, norm_ffn_w, w_up, conv_w, conv_b,
           w_down):
    L = w_ada.shape[0]
    D = x_prompt.shape[-1]
    Bp, Bs = c_prompt.shape[0], c_sample.shape[0]
    rows = -(-(Bp + Bs) // 8) * 8
    c_all = jnp.zeros((rows, D), jnp.float32).at[:Bp].set(c_prompt).at[Bp:Bp + Bs].set(c_sample)
    mod = adaln_mod(c_all, w_ada, b_ada)
    mod_p = mod[:, :Bp].reshape(L, Bp, 6, 1, D)
    mod_s = mod[:, Bp:Bp + Bs].reshape(L, Bs, 6, 1, D)

    lb_cum = jnp.cumsum(jax.nn.softmax(hgrn_lb.astype(jnp.float32), axis=0), axis=0)
    lb_all = lb_cum - lb_cum[0:1]
    log_lb, log_1m_lb, one_m_lb = jnp.log(lb_all), jnp.log1p(-lb_all), 1.0 - lb_all

    layers = []
    for l in range(L):
        def gates(d):
            return (log_lb[l, d][None], log_1m_lb[l, d][None], one_m_lb[l, d][None])
        layers.append(dict(
            norm_attn_w=norm_attn_w[l], gates_fwd=gates(0), gates_bwd=gates(1), hgrn_norm_w=hgrn_norm_w[l],
            q_norm_w=q_norm_w[l], k_norm_w=k_norm_w[l], diff_lambda=diff_lambda[l].astype(jnp.float32),
            diff_subln_w=diff_subln_w[l], norm_ffn_w=norm_ffn_w[l], conv_w=conv_w[l], conv_b=conv_b[l]))
    weights = dict(w_in=w_in.astype(jnp.bfloat16), w_out=w_out.astype(jnp.bfloat16),
                   w_up=w_up.astype(jnp.bfloat16), w_down=w_down.astype(jnp.bfloat16))
    rel_bias = rel_bias.astype(jnp.float32)
    tiles = {}
    outs = []
    for x, m in ((x_prompt, mod_p), (x_sample, mod_s)):
        S = x.shape[1]
        tile = (min(ATTN_TQ, S), min(ATTN_TK, S))
        if tile not in tiles:
            tiles[tile] = rel_bias_tiles(rel_bias, *tile)
        outs.append(_trunk(x, m, layers, weights, tiles[tile]))
    return tuple(outs)
```

```python
import functools
import math

import jax
import jax.numpy as jnp
from jax import lax
from jax.experimental import pallas as pl
from jax.experimental.pallas import tpu as pltpu

D_MODEL = 4096
HA_HEADS = 16
HA_HEAD = 128
HA_WIDTH = HA_HEADS * HA_HEAD
DB_HEADS = 8
DB_HEAD = 128
DB_WIDTH = DB_HEADS * 2 * DB_HEAD
IN_COLS = 5 * HA_WIDTH + 3 * DB_WIDTH
CHUNK = 64
NUM_BUCKETS = 32
EPS = 1e-6
BUCKET_START = (0, 1, 2, 3, 4, 5, 6, 7, 8, 12, 16, 23, 32, 46, 64, 91)

LANES = 128
V7X_VMEM_BYTES = 64 * 1024 * 1024
VMEM_LIMIT = V7X_VMEM_BYTES - 8 * 1024 * 1024

LOG2E = 1.0 / math.log(2.0)
SAFE_SPAN = 80.0
SAFE_LOG2_GAP = 100.0

_NT = (((1,), (1,)), ((), ()))
_TN = (((0,), (0,)), ((), ()))


def _cparams(n_axes):
    return pltpu.CompilerParams(
        dimension_semantics=("arbitrary",) * n_axes, vmem_limit_bytes=VMEM_LIMIT
    )


def _silu(x):
    return x * jax.nn.sigmoid(x)


def _ada_kernel(c_ref, w_ref, b_ref, o_ref):
    @pl.when(pl.program_id(2) == 0)
    def _():
        o_ref[...] = jnp.broadcast_to(b_ref[...], o_ref.shape)

    a = _silu(c_ref[...]).astype(jnp.bfloat16)
    o_ref[...] += jnp.dot(a, w_ref[...].astype(jnp.bfloat16), preferred_element_type=jnp.float32)


def adaln_mod(c_all, w_ada, b_ada, *, tk=1024, tn=2048):
    R, D = c_all.shape
    L, _, N = w_ada.shape
    return pl.pallas_call(
        _ada_kernel,
        grid=(L, N // tn, D // tk),
        in_specs=[
            pl.BlockSpec((R, tk), lambda l, j, k: (0, k)),
            pl.BlockSpec((None, tk, tn), lambda l, j, k: (l, k, j)),
            pl.BlockSpec((None, 1, tn), lambda l, j, k: (l, 0, j)),
        ],
        out_specs=pl.BlockSpec((None, R, tn), lambda l, j, k: (l, 0, j)),
        out_shape=jax.ShapeDtypeStruct((L, R, N), jnp.float32),
        compiler_params=_cparams(3),
        name="adaln_mod",
    )(c_all, w_ada, b_ada.reshape(L, 1, N))


def _normmod_kernel(x_ref, w_ref, sc_ref, sh_ref, o_ref):
    x = x_ref[...]
    ms = jnp.mean(x * x, axis=-1, keepdims=True)
    y = x * lax.rsqrt(ms + EPS) * w_ref[...]
    o_ref[...] = (y * (1.0 + sc_ref[...]) + sh_ref[...]).astype(o_ref.dtype)


def norm_modulate(x, w, mod, shift_idx, scale_idx, *, ts=512):
    B, S, D = x.shape
    ts = min(ts, S)
    return pl.pallas_call(
        _normmod_kernel,
        grid=(B, S // ts),
        in_specs=[
            pl.BlockSpec((None, ts, D), lambda b, i: (b, i, 0)),
            pl.BlockSpec((1, D), lambda b, i: (0, 0)),
            pl.BlockSpec((None, None, 1, D), lambda b, i: (b, scale_idx, 0, 0)),
            pl.BlockSpec((None, None, 1, D), lambda b, i: (b, shift_idx, 0, 0)),
        ],
        out_specs=pl.BlockSpec((None, ts, D), lambda b, i: (b, i, 0)),
        out_shape=jax.ShapeDtypeStruct((B, S, D), jnp.bfloat16),
        compiler_params=_cparams(2),
        name="norm_modulate",
    )(x, w.reshape(1, D), mod, mod)


def _mm_grouped_kernel(a_ref, w_ref, o_ref):
    acc = jnp.dot(a_ref[...], w_ref[...], preferred_element_type=jnp.float32)
    for g in range(o_ref.shape[0]):
        o_ref[g] = acc[:, g * LANES:(g + 1) * LANES].astype(o_ref.dtype)


def matmul_grouped(a, w, layer, out_dtype, *, tm, tn):
    B, S, K = a.shape
    N = w.shape[2]
    tm, tn = min(tm, S), min(tn, N)
    return pl.pallas_call(
        _mm_grouped_kernel,
        grid=(B, S // tm, N // tn),
        in_specs=[
            pl.BlockSpec((None, tm, K), lambda b, i, j: (b, i, 0)),
            pl.BlockSpec((None, K, tn), lambda b, i, j: (layer, 0, j)),
        ],
        out_specs=pl.BlockSpec((None, tn // LANES, tm, LANES), lambda b, i, j: (b, j, i, 0)),
        out_shape=jax.ShapeDtypeStruct((B, N // LANES, S, LANES), out_dtype),
        compiler_params=_cparams(3),
        name="matmul_grouped",
    )(a, w)


def _mm_res_kernel(*refs, n_lhs):
    a_refs = refs[:n_lhs]
    w_ref, x_ref, g_ref, o_ref = refs[n_lhs:]
    acc = None
    k0 = 0
    for a_ref in a_refs:
        kw = a_ref.shape[-1]
        part = jnp.dot(a_ref[...], w_ref[k0:k0 + kw, :], preferred_element_type=jnp.float32)
        acc = part if acc is None else acc + part
        k0 += kw
    o_ref[...] = x_ref[...] + g_ref[...] * acc


def matmul_residual(lhs_parts, w, layer, x, mod, gate_idx, *, tm, tn):
    B, S, N = x.shape
    K = w.shape[1]
    tm, tn = min(tm, S), min(tn, N)
    n_lhs = len(lhs_parts)
    in_specs = [pl.BlockSpec((None, tm, a.shape[-1]), lambda b, i, j: (b, i, 0)) for a in lhs_parts]
    in_specs += [
        pl.BlockSpec((None, K, tn), lambda b, i, j: (layer, 0, j)),
        pl.BlockSpec((None, tm, tn), lambda b, i, j: (b, i, j)),
        pl.BlockSpec((None, None, 1, tn), lambda b, i, j: (b, gate_idx, 0, j)),
    ]
    return pl.pallas_call(
        functools.partial(_mm_res_kernel, n_lhs=n_lhs),
        grid=(B, S // tm, N // tn),
        in_specs=in_specs,
        out_specs=pl.BlockSpec((None, tm, tn), lambda b, i, j: (b, i, j)),
        out_shape=jax.ShapeDtypeStruct((B, S, N), jnp.float32),
        compiler_params=_cparams(3),
        name="matmul_residual",
    )(*lhs_parts, w, x, mod)


HALO = 16


def _upconv_kernel(hp_ref, h_ref, hn_ref, wv_ref, wg_ref, cwv_ref, cwg_ref, cbv_ref, cbg_ref, o_ref,
                   lhs_ref, ua_ref, ub_ref, *, nj):
    i = pl.program_id(1)
    j = pl.program_id(2)
    last_i = pl.num_programs(1) - 1
    tm = h_ref.shape[0]
    tn = wv_ref.shape[1]

    n_slab = 4 if tm % 64 == 0 else 1
    rs = tm // n_slab

    def project(u_ref, s):
        lo = 0 if s == 0 else HALO + s * rs
        hi = tm + 2 * HALO if s == n_slab - 1 else HALO + (s + 1) * rs
        lhs = lhs_ref[lo:hi, :]
        u_ref[lo:hi, 0:tn] = jnp.dot(lhs, wv_ref[...], preferred_element_type=jnp.float32)
        u_ref[lo:hi, tn:2 * tn] = jnp.dot(lhs, wg_ref[...], preferred_element_type=jnp.float32)

    rc = min(rs, 64)

    def conv_gate(u_ref, s):
        def conv(r0, col, lanes, w_ref, b_ref):
            up = u_ref[r0 - 1:r0 - 1 + rc, col:col + LANES]
            cur = u_ref[r0:r0 + rc, col:col + LANES]
            dn = u_ref[r0 + 1:r0 + 1 + rc, col:col + LANES]
            return up * w_ref[0:1, lanes] + cur * w_ref[1:2, lanes] + dn * w_ref[2:3, lanes] + b_ref[:, lanes]

        for lt in range(tn // LANES):
            lanes = slice(lt * LANES, (lt + 1) * LANES)
            for c in range(rs // rc):
                o0 = s * rs + c * rc
                val = conv(HALO + o0, lt * LANES, lanes, cwv_ref, cbv_ref)
                gate = conv(HALO + o0, tn + lt * LANES, lanes, cwg_ref, cbg_ref)
                o_ref[o0:o0 + rc, lanes] = (_silu(gate) * val).astype(o_ref.dtype)

    def step(new_ref, old_ref):
        for s in range(n_slab):
            if new_ref is not None:
                project(new_ref, s)
            if old_ref is not None:
                conv_gate(old_ref, s)

    @pl.when(j == 0)
    def _():
        zeros = jnp.zeros(hp_ref.shape, hp_ref.dtype)
        lhs_ref[0:HALO, :] = jnp.where(i > 0, hp_ref[...], zeros)
        lhs_ref[HALO:HALO + tm, :] = h_ref[...]
        lhs_ref[HALO + tm:, :] = jnp.where(i < last_i, hn_ref[...], zeros)
        step(ua_ref, None)

    @pl.when((j > 0) & (j < nj) & (j % 2 == 1))
    def _():
        step(ub_ref, ua_ref)

    @pl.when((j > 0) & (j < nj) & (j % 2 == 0))
    def _():
        step(ua_ref, ub_ref)

    @pl.when(j == nj)
    def _():
        step(None, ua_ref if (nj - 1) % 2 == 0 else ub_ref)


def up_conv_gate(h, w_up, layer, conv_w, conv_b, *, tm=1024, tn=256):
    B, S, K = h.shape
    F = w_up.shape[2] // 2
    tm, tn = min(tm, S), min(tn, F)
    nj = F // tn
    rh = tm // HALO

    def col(j):
        return jnp.minimum(j, nj - 1)

    def lag(j):
        return jnp.maximum(j - 1, 0)

    cb = conv_b.reshape(1, 2 * F)
    return pl.pallas_call(
        functools.partial(_upconv_kernel, nj=nj),
        grid=(B, S // tm, nj + 1),
        in_specs=[
            pl.BlockSpec((None, HALO, K), lambda b, i, j: (b, jnp.maximum(i * rh - 1, 0), 0)),
            pl.BlockSpec((None, tm, K), lambda b, i, j: (b, i, 0)),
            pl.BlockSpec((None, HALO, K), lambda b, i, j: (b, jnp.minimum((i + 1) * rh, S // HALO - 1), 0)),
            pl.BlockSpec((None, K, tn), lambda b, i, j: (layer, 0, col(j))),
            pl.BlockSpec((None, K, tn), lambda b, i, j: (layer, 0, nj + col(j))),
            pl.BlockSpec((3, tn), lambda b, i, j: (0, lag(j))),
            pl.BlockSpec((3, tn), lambda b, i, j: (0, nj + lag(j))),
            pl.BlockSpec((1, tn), lambda b, i, j: (0, lag(j))),
            pl.BlockSpec((1, tn), lambda b, i, j: (0, nj + lag(j))),
        ],
        out_specs=pl.BlockSpec((None, tm, tn), lambda b, i, j: (b, i, lag(j))),
        out_shape=jax.ShapeDtypeStruct((B, S, F), jnp.bfloat16),
        scratch_shapes=[pltpu.VMEM((tm + 2 * HALO, K), jnp.bfloat16),
                        pltpu.VMEM((tm + 2 * HALO, 2 * tn), jnp.float32),
                        pltpu.VMEM((tm + 2 * HALO, 2 * tn), jnp.float32)],
        compiler_params=_cparams(3),
        name="up_conv_gate",
    )(h, h, h, w_up, w_up, conv_w, conv_w, cb, cb)


def _hgrn_kernel(*refs, reverse, T):
    if reverse:
        (xq_ref, xf_ref, xi_ref, xg_ref, of_ref, llb_ref, l1m_ref, oml_ref, nw_ref, o_ref,
         st_ref, oin_ref, q_ref, k_ref, b_ref) = refs
    else:
        xq_ref, xf_ref, xi_ref, llb_ref, l1m_ref, oml_ref, o_ref, st_ref, oin_ref, q_ref, k_ref, b_ref = refs
    C = CHUNK
    nC = T // C
    half = C // 2
    quarter = C // 4

    @pl.when(pl.program_id(2) == 0)
    def _():
        st_ref[...] = jnp.zeros_like(st_ref)

    xf = xf_ref[...]
    llb, l1m, oml = llb_ref[...], l1m_ref[...], oml_ref[...]
    q = _silu(xq_ref[...])
    v16 = xi_ref[...].astype(jnp.bfloat16)
    e = jnp.exp(-jnp.abs(xf))
    y = l1m + jnp.minimum(xf, 0.0) - jnp.log(1.0 + e)
    logf = jnp.maximum(llb, y) + jnp.log(1.0 + jnp.exp(-jnp.abs(llb - y)))
    kk = oml * jnp.where(xf >= 0.0, e, 1.0) / (1.0 + e)
    span = jnp.sum((-logf).reshape(T // quarter, quarter, LANES), axis=1)
    exact_needed = jnp.max(span) > SAFE_SPAN

    rc = lax.broadcasted_iota(jnp.int32, (T, LANES), 0) & (C - 1)
    b = logf
    for sft in (1, 2, 4, 8, 16, 32):
        if reverse:
            b = b + jnp.where(rc < C - sft, pltpu.roll(b, T - sft, 0), 0.0)
        else:
            b = b + jnp.where(rc >= sft, pltpu.roll(b, sft, 0), 0.0)

    def rows(x, c):
        return x[c * C:(c + 1) * C, :]

    def b_last(bc):
        return bc[0:1, :] if reverse else bc[C - 1:C, :]

    ti = lax.broadcasted_iota(jnp.int32, (C, C), 0)
    si = lax.broadcasted_iota(jnp.int32, (C, C), 1)
    causal = (si >= ti) if reverse else (si <= ti)

    first = lax.broadcasted_iota(jnp.int32, (C, LANES), 0) < half
    q_side = first if reverse else jnp.logical_not(first)
    m0, m1, bnd = (quarter, half + quarter, half) if reverse else (quarter - 1, half + quarter - 1, half - 1)
    zero = jnp.zeros((C, LANES), jnp.bfloat16)

    def intra_factored(c, bc):
        mid = jnp.where(first, bc[m0:m0 + 1, :], bc[m1:m1 + 1, :])
        qa = (rows(q, c) * jnp.exp(bc - mid)).astype(jnp.bfloat16)
        ka = (rows(kk, c) * jnp.exp(mid - bc)).astype(jnp.bfloat16)
        eb = jnp.exp(-jnp.abs(bc - bc[bnd:bnd + 1, :]))
        qb = (rows(q, c) * eb).astype(jnp.bfloat16)
        kb = (rows(kk, c) * eb).astype(jnp.bfloat16)
        qcat = jnp.concatenate([jnp.where(first, qa, zero), jnp.where(first, zero, qa),
                                jnp.where(q_side, qb, zero)], axis=1)
        kcat = jnp.concatenate([jnp.where(first, ka, zero), jnp.where(first, zero, ka),
                                jnp.where(q_side, zero, kb)], axis=1)
        s = lax.dot_general(qcat, kcat, _NT, preferred_element_type=jnp.float32)
        s = jnp.where(causal, s, 0.0).astype(jnp.bfloat16)
        return jnp.dot(s, rows(v16, c), preferred_element_type=jnp.float32)

    def emit(rsl, o):
        if reverse:
            tot = of_ref[rsl, :] + o
            ms = jnp.mean(tot * tot, axis=-1, keepdims=True)
            yn = tot * lax.rsqrt(ms + EPS) * nw_ref[...]
            o_ref[rsl, :] = (yn * _silu(xg_ref[rsl, :])).astype(o_ref.dtype)
        else:
            o_ref[rsl, :] = o

    st = st_ref[...]
    for c in (range(nC - 1, -1, -1) if reverse else range(nC)):
        rsl = slice(c * C, (c + 1) * C)
        bc = rows(b, c)
        bl = b_last(bc)
        qa = (rows(q, c) * jnp.exp(bc)).astype(jnp.bfloat16)
        o_inter = lax.dot_general(qa, st.astype(jnp.bfloat16), _NT, preferred_element_type=jnp.float32)
        oin_ref[rsl, :] = o_inter
        emit(rsl, intra_factored(c, bc) + o_inter)
        kl = (rows(kk, c) * jnp.exp(bl - bc)).astype(jnp.bfloat16)
        st = st * jnp.exp(bl) + lax.dot_general(rows(v16, c), kl, _TN, preferred_element_type=jnp.float32)
    st_ref[...] = st

    @pl.when(exact_needed)
    def _():
        q_ref[...] = q
        k_ref[...] = kk
        b_ref[...] = b
        row = lax.broadcasted_iota(jnp.int32, (C, LANES), 0)

        def per_chunk(c, carry):
            rsl = pl.ds(pl.multiple_of(c * C, C), C)
            qc = q_ref[rsl, :]
            kc = k_ref[rsl, :]
            bc = b_ref[rsl, :]
            vc = xi_ref[rsl, :]

            def pair(d, acc):
                sh = ((C - d) % C) if reverse else d
                valid = (row < C - d) if reverse else (row >= d)
                dec = jnp.exp(jnp.where(valid, bc - pltpu.roll(bc, sh, 0), -jnp.inf))
                w = jnp.sum(qc * pltpu.roll(kc, sh, 0) * dec, axis=-1, keepdims=True)
                return acc + w * pltpu.roll(vc, sh, 0)

            oi = lax.fori_loop(0, C, pair, jnp.zeros((C, LANES), jnp.float32))
            emit(rsl, oi + oin_ref[rsl, :])
            return carry

        lax.fori_loop(0, nC, per_chunk, 0)


def hgrn_pass(zg, gate_params, *, reverse, o_fwd=None, norm_w=None, T=512):
    B, _, S, _ = zg.shape
    T = min(T, S)
    nblk = S // T
    H = HA_HEADS

    def seq(c):
        return (nblk - 1 - c) if reverse else c

    def zspec(section):
        return pl.BlockSpec((None, None, T, LANES), lambda b, h, c: (b, section * H + h, seq(c), 0))

    hspec = pl.BlockSpec((None, None, T, LANES), lambda b, h, c: (b, h, seq(c), 0))
    pspec = pl.BlockSpec((1, LANES), lambda b, h, c: (0, h))
    llb, l1m, oml = gate_params
    if reverse:
        in_specs = [zspec(0), zspec(2), zspec(3), zspec(4), hspec,
                    pspec, pspec, pspec, pl.BlockSpec((1, LANES), lambda b, h, c: (0, 0))]
        args = (zg, zg, zg, zg, o_fwd, llb, l1m, oml, norm_w.reshape(1, LANES))
        out_spec = pl.BlockSpec((None, T, LANES), lambda b, h, c: (b, seq(c), h))
        out_shape = jax.ShapeDtypeStruct((B, S, HA_WIDTH), jnp.bfloat16)
    else:
        in_specs = [zspec(0), zspec(1), zspec(3), pspec, pspec, pspec]
        args = (zg, zg, zg, llb, l1m, oml)
        out_spec = hspec
        out_shape = jax.ShapeDtypeStruct((B, H, S, LANES), jnp.float32)
    blk = pltpu.VMEM((T, LANES), jnp.float32)
    return pl.pallas_call(
        functools.partial(_hgrn_kernel, reverse=reverse, T=T),
        grid=(B, H, nblk),
        in_specs=in_specs,
        out_specs=out_spec,
        out_shape=out_shape,
        scratch_shapes=[pltpu.VMEM((HA_HEAD, HA_HEAD), jnp.float32), blk, blk, blk, blk],
        compiler_params=_cparams(3),
        name="hgrn_bwd" if reverse else "hgrn_fwd",
    )(*args)


def _qknorm_kernel(q_ref, k_ref, v_ref, qw_ref, kw_ref, qo_ref, ko_ref, vo_ref, *, q_scale):
    qw = qw_ref[...] * q_scale
    kw = kw_ref[...]
    for g in range(q_ref.shape[0]):
        h, c = divmod(g, 2)
        sl = slice(c * DB_HEAD, (c + 1) * DB_HEAD)
        for x_ref, w, o_ref in ((q_ref, qw, qo_ref), (k_ref, kw, ko_ref)):
            x = x_ref[g]
            ms = jnp.mean(x * x, axis=-1, keepdims=True)
            o_ref[h, :, sl] = (x * lax.rsqrt(ms + EPS) * w).astype(o_ref.dtype)
        vo_ref[h, :, sl] = v_ref[g].astype(vo_ref.dtype)


def qk_norm(zg, q_norm_w, k_norm_w, *, ts=512):
    B, _, S, _ = zg.shape
    ts = min(ts, S)
    ng = DB_WIDTH // LANES
    first = 5 * HA_WIDTH // DB_WIDTH

    def zspec(sec):
        return pl.BlockSpec((None, ng, ts, LANES), lambda b, i: (b, first + sec, i, 0))

    wspec = pl.BlockSpec((1, DB_HEAD), lambda b, i: (0, 0))
    ospec = pl.BlockSpec((None, DB_HEADS, ts, 2 * DB_HEAD), lambda b, i: (b, 0, i, 0))
    oshape = jax.ShapeDtypeStruct((B, DB_HEADS, S, 2 * DB_HEAD), jnp.bfloat16)
    return pl.pallas_call(
        functools.partial(_qknorm_kernel, q_scale=DB_HEAD ** -0.5 * LOG2E),
        grid=(B, S // ts),
        in_specs=[zspec(0), zspec(1), zspec(2), wspec, wspec],
        out_specs=(ospec, ospec, ospec),
        out_shape=(oshape, oshape, oshape),
        compiler_params=_cparams(2),
        name="qk_norm",
    )(zg, zg, zg, q_norm_w.reshape(1, DB_HEAD), k_norm_w.reshape(1, DB_HEAD))


BIAS_TILES = 5


def _bias_kernel(rb_ref, o_ref, *, t):
    h = pl.program_id(0)
    offset = (pl.program_id(1) - BIAS_TILES // 2) * t
    qi = lax.broadcasted_iota(jnp.int32, (t, t), 0)
    kj = lax.broadcasted_iota(jnp.int32, (t, t), 1)
    rel = kj - qi + offset
    n = jnp.abs(rel)
    half = NUM_BUCKETS // 2

    def side(base):
        val = jnp.full((t, t), rb_ref[base + half - 1, h], jnp.float32)
        for c in range(half - 2, -1, -1):
            val = jnp.where(n < BUCKET_START[c + 1], rb_ref[base + c, h], val)
        return val

    o_ref[...] = jnp.where(rel > 0, side(half), side(0)) * LOG2E


def rel_bias_tiles(rel_bias, t):
    assert t >= 128
    return pl.pallas_call(
        functools.partial(_bias_kernel, t=t),
        grid=(DB_HEADS, BIAS_TILES),
        in_specs=[pl.BlockSpec(memory_space=pltpu.SMEM)],
        out_specs=pl.BlockSpec((None, None, t, t), lambda h, d: (h, d, 0, 0)),
        out_shape=jax.ShapeDtypeStruct((DB_HEADS, BIAS_TILES, t, t), jnp.float32),
        compiler_params=_cparams(2),
        name="rel_bias_tiles",
    )(rel_bias)


ATTN_ROWS = 32


def _attn_kernel(q_ref, k_ref, v_ref, bias_ref, lp_ref, sw_ref, o_ref,
                 m_ref, l_ref, acc_ref, m0_ref, l0_ref, acc0_ref, s_a, s_b, p_a, p_b, al_a, al_b, *, lambda_init):
    qb = pl.program_id(2)
    kb = pl.program_id(3)
    t = q_ref.shape[0]
    n_sub = k_ref.shape[0] // t
    rc = min(ATTN_ROWS, t)
    s_bufs, p_bufs, al_bufs = (s_a, s_b), (p_a, p_b), (al_a, al_b)

    @pl.when(kb == 0)
    def _():
        m_ref[...] = jnp.zeros_like(m_ref)
        l_ref[...] = jnp.zeros_like(l_ref)
        acc_ref[...] = jnp.zeros_like(acc_ref)

    m0_ref[...] = m_ref[...]
    l0_ref[...] = l_ref[...]
    acc0_ref[...] = acc_ref[...]

    def logits_into(j):
        for c in range(2):
            s_bufs[j % 2][c] = lax.dot_general(q_ref[:, c * DB_HEAD:(c + 1) * DB_HEAD],
                                               k_ref[j * t:(j + 1) * t, c * DB_HEAD:(c + 1) * DB_HEAD],
                                               _NT, preferred_element_type=jnp.float32)

    def widen(x, width):
        return jnp.tile(x, (1, width // LANES))

    def sweep(safe):
        excess = jnp.zeros((rc, LANES), jnp.float32)
        logits_into(0)
        for j in range(n_sub):
            if j + 1 < n_sub:
                logits_into(j + 1)
            s_buf, p_buf, al_buf = s_bufs[j % 2], p_bufs[j % 2], al_bufs[j % 2]
            tile_idx = jnp.clip(kb * n_sub + j - qb, -2, 2) + 2
            for i in range(t // rc):
                rows = slice(i * rc, (i + 1) * rc)
                bias = bias_ref[tile_idx, rows, :]
                for c in range(2):
                    crows = slice(c * t + i * rc, c * t + (i + 1) * rc)
                    s = s_buf[c, rows, :] + bias
                    smax = jnp.max(s, axis=-1, keepdims=True)
                    r = m_ref[c, rows, :]
                    m_new = jnp.maximum(r, smax)
                    alpha = jnp.exp2(r - m_new)
                    if safe:
                        p = jnp.exp2(s - widen(m_new, t))
                        l_ref[c, rows, :] = alpha * l_ref[c, rows, :] + jnp.sum(p, axis=-1, keepdims=True)
                    else:
                        p = jnp.exp2(s - widen(r, t))
                        l_ref[c, rows, :] = alpha * (l_ref[c, rows, :] + jnp.sum(p, axis=-1, keepdims=True))
                        gap = smax - r
                        if j == 0:
                            gap = jnp.where(kb == 0, jnp.abs(gap), gap)
                        excess = jnp.maximum(excess, gap)
                    m_ref[c, rows, :] = m_new
                    p_buf[crows, :] = p.astype(jnp.bfloat16)
                    al_buf[crows, :] = alpha
            pv = jnp.dot(p_buf[...], v_ref[j * t:(j + 1) * t, :], preferred_element_type=jnp.float32)
            a2 = widen(al_buf[...], 2 * LANES)
            acc_ref[...] = (a2 * acc_ref[...] + pv) if safe else (a2 * (acc_ref[...] + pv))
        return excess

    redo = jnp.max(sweep(safe=False)) > SAFE_LOG2_GAP

    @pl.when(redo)
    def _():
        m_ref[...] = m0_ref[...] + jnp.where(kb == 0, -jnp.inf, 0.0)
        l_ref[...] = l0_ref[...]
        acc_ref[...] = acc0_ref[...]
        sweep(safe=True)

    @pl.when(kb == pl.num_programs(3) - 1)
    def _():
        lp = lp_ref[...]
        lam = (jnp.exp(jnp.sum(lp[0:1, :] * lp[1:2, :], axis=-1, keepdims=True))
               - jnp.exp(jnp.sum(lp[2:3, :] * lp[3:4, :], axis=-1, keepdims=True)) + lambda_init)
        o = (acc_ref[0:t, :] / jnp.tile(l_ref[0], (1, 2))
             - lam * (acc_ref[t:2 * t, :] / jnp.tile(l_ref[1], (1, 2))))
        ms = jnp.mean(o * o, axis=-1, keepdims=True)
        o_ref[...] = (o * lax.rsqrt(ms + EPS) * sw_ref[...] * (1.0 - lambda_init)).astype(o_ref.dtype)


def diff_attention(qn, kn, vb, bias_tiles, lam_params, subln_w, lambda_init):
    B, H, S, W = qn.shape
    t = bias_tiles.shape[-1]
    tk = min(ATTN_KEYS_PER_STEP, S)
    stat = pltpu.VMEM((2, t, LANES), jnp.float32)
    accs = pltpu.VMEM((2 * t, W), jnp.float32)
    return pl.pallas_call(
        functools.partial(_attn_kernel, lambda_init=lambda_init),
        grid=(B, H, S // t, S // tk),
        in_specs=[
            pl.BlockSpec((None, None, t, W), lambda b, h, i, j: (b, h, i, 0)),
            pl.BlockSpec((None, None, tk, W), lambda b, h, i, j: (b, h, j, 0)),
            pl.BlockSpec((None, None, tk, W), lambda b, h, i, j: (b, h, j, 0)),
            pl.BlockSpec((None, BIAS_TILES, t, t), lambda b, h, i, j: (h, 0, 0, 0)),
            pl.BlockSpec((4, DB_HEAD), lambda b, h, i, j: (0, 0)),
            pl.BlockSpec((1, W), lambda b, h, i, j: (0, 0)),
        ],
        out_specs=pl.BlockSpec((None, t, W), lambda b, h, i, j: (b, i, h)),
        out_shape=jax.ShapeDtypeStruct((B, S, DB_WIDTH), jnp.bfloat16),
        scratch_shapes=[stat, stat, accs, stat, stat, accs,
                        pltpu.VMEM((2, t, t), jnp.float32), pltpu.VMEM((2, t, t), jnp.float32),
                        pltpu.VMEM((2 * t, t), jnp.bfloat16), pltpu.VMEM((2 * t, t), jnp.bfloat16),
                        pltpu.VMEM((2 * t, LANES), jnp.float32), pltpu.VMEM((2 * t, LANES), jnp.float32)],
        compiler_params=_cparams(4),
        name="diff_attention",
    )(qn, kn, vb, bias_tiles, lam_params, subln_w.reshape(1, W))


ATTN_TILE = 512
ATTN_KEYS_PER_STEP = 2048


def _trunk(x, mod, layers, weights, bias_tiles):
    for l, p in enumerate(layers):
        m = mod[l]
        lambda_init = 0.8 - 0.6 * math.exp(-0.3 * l)
        h = norm_modulate(x, p["norm_attn_w"], m, 0, 1)
        zg = matmul_grouped(h, weights["w_in"], l, jnp.float32, tm=1024, tn=1024)
        o_fwd = hgrn_pass(zg, p["gates_fwd"], reverse=False)
        out_a = hgrn_pass(zg, p["gates_bwd"], reverse=True, o_fwd=o_fwd, norm_w=p["hgrn_norm_w"])
        qn, kn, vb = qk_norm(zg, p["q_norm_w"], p["k_norm_w"])
        out_b = diff_attention(qn, kn, vb, bias_tiles, p["diff_lambda"], p["diff_subln_w"], lambda_init)
        x = matmul_residual([out_a, out_b], weights["w_out"], l, x, m, 2, tm=1024, tn=1024)
        h = norm_modulate(x, p["norm_ffn_w"], m, 3, 4)
        act = up_conv_gate(h, weights["w_up"], l, p["conv_w"], p["conv_b"])
        x = matmul_residual([act], weights["w_down"], l, x, m, 5, tm=512, tn=512)
    return x


def kernel(x_prompt, x_sample, c_prompt, c_sample, w_ada, b_ada, norm_attn_w, w_in, hgrn_lb, hgrn_norm_w,
           q_norm_w, k_norm_w, diff_lambda, diff_subln_w, rel_bias, w_out, norm_ffn_w, w_up, conv_w, conv_b,
           w_down):
    L = w_ada.shape[0]
    D = x_prompt.shape[-1]
    Bp, Bs = c_prompt.shape[0], c_sample.shape[0]
    rows = -(-(Bp + Bs) // 8) * 8
    c_all = jnp.zeros((rows, D), jnp.float32).at[:Bp].set(c_prompt).at[Bp:Bp + Bs].set(c_sample)
    mod = adaln_mod(c_all, w_ada, b_ada)
    mod_p = mod[:, :Bp].reshape(L, Bp, 6, 1, D)
    mod_s = mod[:, Bp:Bp + Bs].reshape(L, Bs, 6, 1, D)

    lb_cum = jnp.cumsum(jax.nn.softmax(hgrn_lb.astype(jnp.float32), axis=0), axis=0)
    lb_all = lb_cum - lb_cum[0:1]
    log_lb, log_1m_lb, one_m_lb = jnp.log(lb_all), jnp.log1p(-lb_all), 1.0 - lb_all

    layers = []
    for l in range(L):
        def gates(d):
            return (log_lb[l, d][None], log_1m_lb[l, d][None], one_m_lb[l, d][None])
        layers.append(dict(
            norm_attn_w=norm_attn_w[l], gates_fwd=gates(0), gates_bwd=gates(1), hgrn_norm_w=hgrn_norm_w[l],
            q_norm_w=q_norm_w[l], k_norm_w=k_norm_w[l], diff_lambda=diff_lambda[l].astype(jnp.float32),
            diff_subln_w=diff_subln_w[l], norm_ffn_w=norm_ffn_w[l], conv_w=conv_w[l], conv_b=conv_b[l]))
    weights = dict(w_in=w_in.astype(jnp.bfloat16), w_out=w_out.astype(jnp.bfloat16),
                   w_up=w_up.astype(jnp.bfloat16), w_down=w_down.astype(jnp.bfloat16))
    rel_bias = rel_bias.astype(jnp.float32)
    tiles = {}
    outs = []
    for x, m in ((x_prompt, mod_p), (x_sample, mod_s)):
        S = x.shape[1]
        tile = min(ATTN_TILE, S)
        if tile not in tiles:
            tiles[tile] = rel_bias_tiles(rel_bias, tile)
        outs.append(_trunk(x, m, layers, weights, tiles[tile]))
    return tuple(outs)
```

```python
import functools
import math

import jax
import jax.numpy as jnp
from jax import lax
from jax.experimental import pallas as pl
from jax.experimental.pallas import tpu as pltpu

D_MODEL = 4096
HA_HEADS = 16
HA_HEAD = 128
HA_WIDTH = HA_HEADS * HA_HEAD
DB_HEADS = 8
DB_HEAD = 128
DB_WIDTH = DB_HEADS * 2 * DB_HEAD
IN_COLS = 5 * HA_WIDTH + 3 * DB_WIDTH
CHUNK = 64
NUM_BUCKETS = 32
EPS = 1e-6
BUCKET_START = (0, 1, 2, 3, 4, 5, 6, 7, 8, 12, 16, 23, 32, 46, 64, 91)

LANES = 128
V7X_VMEM_BYTES = 64 * 1024 * 1024
VMEM_LIMIT = V7X_VMEM_BYTES - 8 * 1024 * 1024

LOG2E = 1.0 / math.log(2.0)
SAFE_SPAN = 80.0
SAFE_LOG2_GAP = 100.0

_NT = (((1,), (1,)), ((), ()))
_TN = (((0,), (0,)), ((), ()))


def _cparams(n_axes):
    return pltpu.CompilerParams(
        dimension_semantics=("arbitrary",) * n_axes, vmem_limit_bytes=VMEM_LIMIT
    )


def _silu(x):
    return x * jax.nn.sigmoid(x)


def _ada_kernel(c_ref, w_ref, b_ref, o_ref):
    @pl.when(pl.program_id(2) == 0)
    def _():
        o_ref[...] = jnp.broadcast_to(b_ref[...], o_ref.shape)

    a = _silu(c_ref[...]).astype(jnp.bfloat16)
    o_ref[...] += jnp.dot(a, w_ref[...].astype(jnp.bfloat16), preferred_element_type=jnp.float32)


def adaln_mod(c_all, w_ada, b_ada, *, tk=1024, tn=2048):
    R, D = c_all.shape
    L, _, N = w_ada.shape
    return pl.pallas_call(
        _ada_kernel,
        grid=(L, N // tn, D // tk),
        in_specs=[
            pl.BlockSpec((R, tk), lambda l, j, k: (0, k)),
            pl.BlockSpec((None, tk, tn), lambda l, j, k: (l, k, j)),
            pl.BlockSpec((None, 1, tn), lambda l, j, k: (l, 0, j)),
        ],
        out_specs=pl.BlockSpec((None, R, tn), lambda l, j, k: (l, 0, j)),
        out_shape=jax.ShapeDtypeStruct((L, R, N), jnp.float32),
        compiler_params=_cparams(3),
        name="adaln_mod",
    )(c_all, w_ada, b_ada.reshape(L, 1, N))


def _normmod_kernel(x_ref, w_ref, sc_ref, sh_ref, o_ref):
    x = x_ref[...]
    ms = jnp.mean(x * x, axis=-1, keepdims=True)
    y = x * lax.rsqrt(ms + EPS) * w_ref[...]
    o_ref[...] = (y * (1.0 + sc_ref[...]) + sh_ref[...]).astype(o_ref.dtype)


def norm_modulate(x, w, mod, shift_idx, scale_idx, *, ts=512):
    B, S, D = x.shape
    ts = min(ts, S)
    return pl.pallas_call(
        _normmod_kernel,
        grid=(B, S // ts),
        in_specs=[
            pl.BlockSpec((None, ts, D), lambda b, i: (b, i, 0)),
            pl.BlockSpec((1, D), lambda b, i: (0, 0)),
            pl.BlockSpec((None, None, 1, D), lambda b, i: (b, scale_idx, 0, 0)),
            pl.BlockSpec((None, None, 1, D), lambda b, i: (b, shift_idx, 0, 0)),
        ],
        out_specs=pl.BlockSpec((None, ts, D), lambda b, i: (b, i, 0)),
        out_shape=jax.ShapeDtypeStruct((B, S, D), jnp.bfloat16),
        compiler_params=_cparams(2),
        name="norm_modulate",
    )(x, w.reshape(1, D), mod, mod)


def _mm_grouped_kernel(a_ref, w_ref, o_ref):
    acc = jnp.dot(a_ref[...], w_ref[...], preferred_element_type=jnp.float32)
    for g in range(o_ref.shape[0]):
        o_ref[g] = acc[:, g * LANES:(g + 1) * LANES].astype(o_ref.dtype)


def matmul_grouped(a, w, layer, out_dtype, *, tm, tn):
    B, S, K = a.shape
    N = w.shape[2]
    tm, tn = min(tm, S), min(tn, N)
    return pl.pallas_call(
        _mm_grouped_kernel,
        grid=(B, S // tm, N // tn),
        in_specs=[
            pl.BlockSpec((None, tm, K), lambda b, i, j: (b, i, 0)),
            pl.BlockSpec((None, K, tn), lambda b, i, j: (layer, 0, j)),
        ],
        out_specs=pl.BlockSpec((None, tn // LANES, tm, LANES), lambda b, i, j: (b, j, i, 0)),
        out_shape=jax.ShapeDtypeStruct((B, N // LANES, S, LANES), out_dtype),
        compiler_params=_cparams(3),
        name="matmul_grouped",
    )(a, w)


def _mm_res_kernel(*refs, n_lhs):
    a_refs = refs[:n_lhs]
    w_ref, x_ref, g_ref, o_ref = refs[n_lhs:]
    acc = None
    k0 = 0
    for a_ref in a_refs:
        kw = a_ref.shape[-1]
        part = jnp.dot(a_ref[...], w_ref[k0:k0 + kw, :], preferred_element_type=jnp.float32)
        acc = part if acc is None else acc + part
        k0 += kw
    o_ref[...] = x_ref[...] + g_ref[...] * acc


def matmul_residual(lhs_parts, w, layer, x, mod, gate_idx, *, tm, tn):
    B, S, N = x.shape
    K = w.shape[1]
    tm, tn = min(tm, S), min(tn, N)
    n_lhs = len(lhs_parts)
    in_specs = [pl.BlockSpec((None, tm, a.shape[-1]), lambda b, i, j: (b, i, 0)) for a in lhs_parts]
    in_specs += [
        pl.BlockSpec((None, K, tn), lambda b, i, j: (layer, 0, j)),
        pl.BlockSpec((None, tm, tn), lambda b, i, j: (b, i, j)),
        pl.BlockSpec((None, None, 1, tn), lambda b, i, j: (b, gate_idx, 0, j)),
    ]
    return pl.pallas_call(
        functools.partial(_mm_res_kernel, n_lhs=n_lhs),
        grid=(B, S // tm, N // tn),
        in_specs=in_specs,
        out_specs=pl.BlockSpec((None, tm, tn), lambda b, i, j: (b, i, j)),
        out_shape=jax.ShapeDtypeStruct((B, S, N), jnp.float32),
        compiler_params=_cparams(3),
        name="matmul_residual",
    )(*lhs_parts, w, x, mod)


HALO = 16


def _upconv_kernel(hp_ref, h_ref, hn_ref, wv_ref, wg_ref, cwv_ref, cwg_ref, cbv_ref, cbg_ref, o_ref,
                   lhs_ref, ua_ref, ub_ref, *, nj):
    i = pl.program_id(1)
    j = pl.program_id(2)
    last_i = pl.num_programs(1) - 1
    tm = h_ref.shape[0]
    tn = wv_ref.shape[1]

    n_slab = 4 if (tm % 64 == 0 and lhs_ref.shape[1] % 1024 == 0) else 1
    rs = tm // n_slab
    kq = lhs_ref.shape[1] // n_slab

    def project(u_ref, s):
        ksl = slice(s * kq, (s + 1) * kq)
        lhs = lhs_ref[:, ksl]
        for col, w_ref in ((0, wv_ref), (tn, wg_ref)):
            part = jnp.dot(lhs, w_ref[ksl, :], preferred_element_type=jnp.float32)
            if s == 0:
                u_ref[:, col:col + tn] = part
            else:
                u_ref[:, col:col + tn] += part

    rc = min(rs, 64)

    def conv_gate(u_ref, s):
        def conv(r0, col, lanes, w_ref, b_ref):
            up = u_ref[r0 - 1:r0 - 1 + rc, col:col + LANES]
            cur = u_ref[r0:r0 + rc, col:col + LANES]
            dn = u_ref[r0 + 1:r0 + 1 + rc, col:col + LANES]
            return up * w_ref[0:1, lanes] + cur * w_ref[1:2, lanes] + dn * w_ref[2:3, lanes] + b_ref[:, lanes]

        for lt in range(tn // LANES):
            lanes = slice(lt * LANES, (lt + 1) * LANES)
            for c in range(rs // rc):
                o0 = s * rs + c * rc
                val = conv(HALO + o0, lt * LANES, lanes, cwv_ref, cbv_ref)
                gate = conv(HALO + o0, tn + lt * LANES, lanes, cwg_ref, cbg_ref)
                o_ref[o0:o0 + rc, lanes] = (_silu(gate) * val).astype(o_ref.dtype)

    def step(new_ref, old_ref):
        for s in range(n_slab):
            if new_ref is not None:
                project(new_ref, s)
            if old_ref is not None:
                conv_gate(old_ref, s)

    @pl.when(j == 0)
    def _():
        zeros = jnp.zeros(hp_ref.shape, hp_ref.dtype)
        lhs_ref[0:HALO, :] = jnp.where(i > 0, hp_ref[...], zeros)
        lhs_ref[HALO:HALO + tm, :] = h_ref[...]
        lhs_ref[HALO + tm:, :] = jnp.where(i < last_i, hn_ref[...], zeros)
        step(ua_ref, None)

    @pl.when((j > 0) & (j < nj) & (j % 2 == 1))
    def _():
        step(ub_ref, ua_ref)

    @pl.when((j > 0) & (j < nj) & (j % 2 == 0))
    def _():
        step(ua_ref, ub_ref)

    @pl.when(j == nj)
    def _():
        step(None, ua_ref if (nj - 1) % 2 == 0 else ub_ref)


UP_TN = 256


def tile_columns(w, tn):
    L, K, N = w.shape
    return w.reshape(L, K, N // tn, tn).transpose(0, 2, 1, 3)


def up_conv_gate(h, w_up_tiles, layer, conv_w, conv_b, *, tm=1024):
    B, S, K = h.shape
    tn = w_up_tiles.shape[3]
    F = w_up_tiles.shape[1] * tn // 2
    tm = min(tm, S)
    nj = F // tn
    rh = tm // HALO

    def col(j):
        return jnp.minimum(j, nj - 1)

    def lag(j):
        return jnp.maximum(j - 1, 0)

    cb = conv_b.reshape(1, 2 * F)
    return pl.pallas_call(
        functools.partial(_upconv_kernel, nj=nj),
        grid=(B, S // tm, nj + 1),
        in_specs=[
            pl.BlockSpec((None, HALO, K), lambda b, i, j: (b, jnp.maximum(i * rh - 1, 0), 0)),
            pl.BlockSpec((None, tm, K), lambda b, i, j: (b, i, 0)),
            pl.BlockSpec((None, HALO, K), lambda b, i, j: (b, jnp.minimum((i + 1) * rh, S // HALO - 1), 0)),
            pl.BlockSpec((None, None, K, tn), lambda b, i, j: (layer, col(j), 0, 0)),
            pl.BlockSpec((None, None, K, tn), lambda b, i, j: (layer, nj + col(j), 0, 0)),
            pl.BlockSpec((3, tn), lambda b, i, j: (0, lag(j))),
            pl.BlockSpec((3, tn), lambda b, i, j: (0, nj + lag(j))),
            pl.BlockSpec((1, tn), lambda b, i, j: (0, lag(j))),
            pl.BlockSpec((1, tn), lambda b, i, j: (0, nj + lag(j))),
        ],
        out_specs=pl.BlockSpec((None, tm, tn), lambda b, i, j: (b, i, lag(j))),
        out_shape=jax.ShapeDtypeStruct((B, S, F), jnp.bfloat16),
        scratch_shapes=[pltpu.VMEM((tm + 2 * HALO, K), jnp.bfloat16),
                        pltpu.VMEM((tm + 2 * HALO, 2 * tn), jnp.float32),
                        pltpu.VMEM((tm + 2 * HALO, 2 * tn), jnp.float32)],
        compiler_params=_cparams(3),
        name="up_conv_gate",
    )(h, h, h, w_up_tiles, w_up_tiles, conv_w, conv_w, cb, cb)


def _hgrn_kernel(*refs, reverse, T):
    if reverse:
        (xq_ref, xf_ref, xi_ref, xg_ref, of_ref, llb_ref, l1m_ref, oml_ref, nw_ref, o_ref,
         st_ref, oin_ref, q_ref, k_ref, b_ref) = refs
    else:
        xq_ref, xf_ref, xi_ref, llb_ref, l1m_ref, oml_ref, o_ref, st_ref, oin_ref, q_ref, k_ref, b_ref = refs
    C = CHUNK
    nC = T // C
    half = C // 2
    quarter = C // 4

    @pl.when(pl.program_id(2) == 0)
    def _():
        st_ref[...] = jnp.zeros_like(st_ref)

    xf = xf_ref[...]
    llb, l1m, oml = llb_ref[...], l1m_ref[...], oml_ref[...]
    q = _silu(xq_ref[...])
    v16 = xi_ref[...].astype(jnp.bfloat16)
    e = jnp.exp(-jnp.abs(xf))
    y = l1m + jnp.minimum(xf, 0.0) - jnp.log(1.0 + e)
    logf = jnp.maximum(llb, y) + jnp.log(1.0 + jnp.exp(-jnp.abs(llb - y)))
    kk = oml * jnp.where(xf >= 0.0, e, 1.0) / (1.0 + e)
    span = jnp.sum((-logf).reshape(T // quarter, quarter, LANES), axis=1)
    exact_needed = jnp.max(span) > SAFE_SPAN

    rc = lax.broadcasted_iota(jnp.int32, (T, LANES), 0) & (C - 1)
    b = logf
    for sft in (1, 2, 4, 8, 16, 32):
        if reverse:
            b = b + jnp.where(rc < C - sft, pltpu.roll(b, T - sft, 0), 0.0)
        else:
            b = b + jnp.where(rc >= sft, pltpu.roll(b, sft, 0), 0.0)

    def rows(x, c):
        return x[c * C:(c + 1) * C, :]

    def b_last(bc):
        return bc[0:1, :] if reverse else bc[C - 1:C, :]

    ti = lax.broadcasted_iota(jnp.int32, (C, C), 0)
    si = lax.broadcasted_iota(jnp.int32, (C, C), 1)
    causal = (si >= ti) if reverse else (si <= ti)

    first = lax.broadcasted_iota(jnp.int32, (C, LANES), 0) < half
    q_side = first if reverse else jnp.logical_not(first)
    m0, m1, bnd = (quarter, half + quarter, half) if reverse else (quarter - 1, half + quarter - 1, half - 1)
    zero = jnp.zeros((C, LANES), jnp.bfloat16)

    def intra_factored(c, bc):
        mid = jnp.where(first, bc[m0:m0 + 1, :], bc[m1:m1 + 1, :])
        qa = (rows(q, c) * jnp.exp(bc - mid)).astype(jnp.bfloat16)
        ka = (rows(kk, c) * jnp.exp(mid - bc)).astype(jnp.bfloat16)
        eb = jnp.exp(-jnp.abs(bc - bc[bnd:bnd + 1, :]))
        qb = (rows(q, c) * eb).astype(jnp.bfloat16)
        kb = (rows(kk, c) * eb).astype(jnp.bfloat16)
        qcat = jnp.concatenate([jnp.where(first, qa, zero), jnp.where(first, zero, qa),
                                jnp.where(q_side, qb, zero)], axis=1)
        kcat = jnp.concatenate([jnp.where(first, ka, zero), jnp.where(first, zero, ka),
                                jnp.where(q_side, zero, kb)], axis=1)
        s = lax.dot_general(qcat, kcat, _NT, preferred_element_type=jnp.float32)
        s = jnp.where(causal, s, 0.0).astype(jnp.bfloat16)
        return jnp.dot(s, rows(v16, c), preferred_element_type=jnp.float32)

    def emit(rsl, o):
        if reverse:
            tot = of_ref[rsl, :] + o
            ms = jnp.mean(tot * tot, axis=-1, keepdims=True)
            yn = tot * lax.rsqrt(ms + EPS) * nw_ref[...]
            o_ref[rsl, :] = (yn * _silu(xg_ref[rsl, :])).astype(o_ref.dtype)
        else:
            o_ref[rsl, :] = o

    st = st_ref[...]
    for c in (range(nC - 1, -1, -1) if reverse else range(nC)):
        rsl = slice(c * C, (c + 1) * C)
        bc = rows(b, c)
        bl = b_last(bc)
        qa = (rows(q, c) * jnp.exp(bc)).astype(jnp.bfloat16)
        o_inter = lax.dot_general(qa, st.astype(jnp.bfloat16), _NT, preferred_element_type=jnp.float32)
        oin_ref[rsl, :] = o_inter
        emit(rsl, intra_factored(c, bc) + o_inter)
        kl = (rows(kk, c) * jnp.exp(bl - bc)).astype(jnp.bfloat16)
        st = st * jnp.exp(bl) + lax.dot_general(rows(v16, c), kl, _TN, preferred_element_type=jnp.float32)
    st_ref[...] = st

    @pl.when(exact_needed)
    def _():
        q_ref[...] = q
        k_ref[...] = kk
        b_ref[...] = b
        row = lax.broadcasted_iota(jnp.int32, (C, LANES), 0)

        def per_chunk(c, carry):
            rsl = pl.ds(pl.multiple_of(c * C, C), C)
            qc = q_ref[rsl, :]
            kc = k_ref[rsl, :]
            bc = b_ref[rsl, :]
            vc = xi_ref[rsl, :]

            def pair(d, acc):
                sh = ((C - d) % C) if reverse else d
                valid = (row < C - d) if reverse else (row >= d)
                dec = jnp.exp(jnp.where(valid, bc - pltpu.roll(bc, sh, 0), -jnp.inf))
                w = jnp.sum(qc * pltpu.roll(kc, sh, 0) * dec, axis=-1, keepdims=True)
                return acc + w * pltpu.roll(vc, sh, 0)

            oi = lax.fori_loop(0, C, pair, jnp.zeros((C, LANES), jnp.float32))
            emit(rsl, oi + oin_ref[rsl, :])
            return carry

        lax.fori_loop(0, nC, per_chunk, 0)


def hgrn_pass(zg, gate_params, *, reverse, o_fwd=None, norm_w=None, T=512):
    B, _, S, _ = zg.shape
    T = min(T, S)
    nblk = S // T
    H = HA_HEADS

    def seq(c):
        return (nblk - 1 - c) if reverse else c

    def zspec(section):
        return pl.BlockSpec((None, None, T, LANES), lambda b, h, c: (b, section * H + h, seq(c), 0))

    hspec = pl.BlockSpec((None, None, T, LANES), lambda b, h, c: (b, h, seq(c), 0))
    pspec = pl.BlockSpec((1, LANES), lambda b, h, c: (0, h))
    llb, l1m, oml = gate_params
    if reverse:
        in_specs = [zspec(0), zspec(2), zspec(3), zspec(4), hspec,
                    pspec, pspec, pspec, pl.BlockSpec((1, LANES), lambda b, h, c: (0, 0))]
        args = (zg, zg, zg, zg, o_fwd, llb, l1m, oml, norm_w.reshape(1, LANES))
        out_spec = pl.BlockSpec((None, T, LANES), lambda b, h, c: (b, seq(c), h))
        out_shape = jax.ShapeDtypeStruct((B, S, HA_WIDTH), jnp.bfloat16)
    else:
        in_specs = [zspec(0), zspec(1), zspec(3), pspec, pspec, pspec]
        args = (zg, zg, zg, llb, l1m, oml)
        out_spec = hspec
        out_shape = jax.ShapeDtypeStruct((B, H, S, LANES), jnp.float32)
    blk = pltpu.VMEM((T, LANES), jnp.float32)
    return pl.pallas_call(
        functools.partial(_hgrn_kernel, reverse=reverse, T=T),
        grid=(B, H, nblk),
        in_specs=in_specs,
        out_specs=out_spec,
        out_shape=out_shape,
        scratch_shapes=[pltpu.VMEM((HA_HEAD, HA_HEAD), jnp.float32), blk, blk, blk, blk],
        compiler_params=_cparams(3),
        name="hgrn_bwd" if reverse else "hgrn_fwd",
    )(*args)


def _qknorm_kernel(q_ref, k_ref, v_ref, qw_ref, kw_ref, qo_ref, ko_ref, vo_ref, *, q_scale):
    qw = qw_ref[...] * q_scale
    kw = kw_ref[...]
    for g in range(q_ref.shape[0]):
        h, c = divmod(g, 2)
        sl = slice(c * DB_HEAD, (c + 1) * DB_HEAD)
        for x_ref, w, o_ref in ((q_ref, qw, qo_ref), (k_ref, kw, ko_ref)):
            x = x_ref[g]
            ms = jnp.mean(x * x, axis=-1, keepdims=True)
            o_ref[h, :, sl] = (x * lax.rsqrt(ms + EPS) * w).astype(o_ref.dtype)
        vo_ref[h, :, sl] = v_ref[g].astype(vo_ref.dtype)


def qk_norm(zg, q_norm_w, k_norm_w, *, ts=512):
    B, _, S, _ = zg.shape
    ts = min(ts, S)
    ng = DB_WIDTH // LANES
    first = 5 * HA_WIDTH // DB_WIDTH

    def zspec(sec):
        return pl.BlockSpec((None, ng, ts, LANES), lambda b, i: (b, first + sec, i, 0))

    wspec = pl.BlockSpec((1, DB_HEAD), lambda b, i: (0, 0))
    ospec = pl.BlockSpec((None, DB_HEADS, ts, 2 * DB_HEAD), lambda b, i: (b, 0, i, 0))
    oshape = jax.ShapeDtypeStruct((B, DB_HEADS, S, 2 * DB_HEAD), jnp.bfloat16)
    return pl.pallas_call(
        functools.partial(_qknorm_kernel, q_scale=DB_HEAD ** -0.5 * LOG2E),
        grid=(B, S // ts),
        in_specs=[zspec(0), zspec(1), zspec(2), wspec, wspec],
        out_specs=(ospec, ospec, ospec),
        out_shape=(oshape, oshape, oshape),
        compiler_params=_cparams(2),
        name="qk_norm",
    )(zg, zg, zg, q_norm_w.reshape(1, DB_HEAD), k_norm_w.reshape(1, DB_HEAD))


BIAS_TILES = 5


def _bias_kernel(rb_ref, o_ref, *, t):
    h = pl.program_id(0)
    offset = (pl.program_id(1) - BIAS_TILES // 2) * t
    qi = lax.broadcasted_iota(jnp.int32, (t, t), 0)
    kj = lax.broadcasted_iota(jnp.int32, (t, t), 1)
    rel = kj - qi + offset
    n = jnp.abs(rel)
    half = NUM_BUCKETS // 2

    def side(base):
        val = jnp.full((t, t), rb_ref[base + half - 1, h], jnp.float32)
        for c in range(half - 2, -1, -1):
            val = jnp.where(n < BUCKET_START[c + 1], rb_ref[base + c, h], val)
        return val

    o_ref[...] = jnp.where(rel > 0, side(half), side(0)) * LOG2E


def rel_bias_tiles(rel_bias, t):
    assert t >= 128
    return pl.pallas_call(
        functools.partial(_bias_kernel, t=t),
        grid=(DB_HEADS, BIAS_TILES),
        in_specs=[pl.BlockSpec(memory_space=pltpu.SMEM)],
        out_specs=pl.BlockSpec((None, None, t, t), lambda h, d: (h, d, 0, 0)),
        out_shape=jax.ShapeDtypeStruct((DB_HEADS, BIAS_TILES, t, t), jnp.float32),
        compiler_params=_cparams(2),
        name="rel_bias_tiles",
    )(rel_bias)


ATTN_ROWS = 32


def _attn_kernel(q_ref, k_ref, v_ref, bias_ref, lp_ref, sw_ref, o_ref,
                 m_ref, l_ref, acc_ref, m0_ref, l0_ref, acc0_ref, s_a, s_b, p_a, p_b, al_a, al_b, *, lambda_init):
    qb = pl.program_id(2)
    kb = pl.program_id(3)
    t = q_ref.shape[0]
    n_sub = k_ref.shape[0] // t
    rc = min(ATTN_ROWS, t)
    s_bufs, p_bufs, al_bufs = (s_a, s_b), (p_a, p_b), (al_a, al_b)

    @pl.when(kb == 0)
    def _():
        m_ref[...] = jnp.zeros_like(m_ref)
        l_ref[...] = jnp.zeros_like(l_ref)
        acc_ref[...] = jnp.zeros_like(acc_ref)

    m0_ref[...] = m_ref[...]
    l0_ref[...] = l_ref[...]
    acc0_ref[...] = acc_ref[...]

    def logits_into(j):
        for c in range(2):
            s_bufs[j % 2][c] = lax.dot_general(q_ref[:, c * DB_HEAD:(c + 1) * DB_HEAD],
                                               k_ref[j * t:(j + 1) * t, c * DB_HEAD:(c + 1) * DB_HEAD],
                                               _NT, preferred_element_type=jnp.float32)

    def widen(x, width):
        return jnp.tile(x, (1, width // LANES))

    def sweep(safe):
        excess = jnp.zeros((rc, LANES), jnp.float32)
        logits_into(0)
        for j in range(n_sub):
            if j + 1 < n_sub:
                logits_into(j + 1)
            s_buf, p_buf, al_buf = s_bufs[j % 2], p_bufs[j % 2], al_bufs[j % 2]
            tile_idx = jnp.clip(kb * n_sub + j - qb, -2, 2) + 2
            for i in range(t // rc):
                rows = slice(i * rc, (i + 1) * rc)
                bias = bias_ref[tile_idx, rows, :]
                for c in range(2):
                    crows = slice(c * t + i * rc, c * t + (i + 1) * rc)
                    s = s_buf[c, rows, :] + bias
                    smax = jnp.max(s, axis=-1, keepdims=True)
                    r = m_ref[c, rows, :]
                    m_new = jnp.maximum(r, smax)
                    alpha = jnp.exp2(r - m_new)
                    if safe:
                        p = jnp.exp2(s - widen(m_new, t))
                        l_ref[c, rows, :] = alpha * l_ref[c, rows, :] + jnp.sum(p, axis=-1, keepdims=True)
                    else:
                        p = jnp.exp2(s - widen(r, t))
                        l_ref[c, rows, :] = alpha * (l_ref[c, rows, :] + jnp.sum(p, axis=-1, keepdims=True))
                        gap = smax - r
                        if j == 0:
                            gap = jnp.where(kb == 0, jnp.abs(gap), gap)
                        excess = jnp.maximum(excess, gap)
                    m_ref[c, rows, :] = m_new
                    p_buf[crows, :] = p.astype(jnp.bfloat16)
                    al_buf[crows, :] = alpha
            pv = jnp.dot(p_buf[...], v_ref[j * t:(j + 1) * t, :], preferred_element_type=jnp.float32)
            a2 = widen(al_buf[...], 2 * LANES)
            acc_ref[...] = (a2 * acc_ref[...] + pv) if safe else (a2 * (acc_ref[...] + pv))
        return excess

    redo = jnp.max(sweep(safe=False)) > SAFE_LOG2_GAP

    @pl.when(redo)
    def _():
        m_ref[...] = m0_ref[...] + jnp.where(kb == 0, -jnp.inf, 0.0)
        l_ref[...] = l0_ref[...]
        acc_ref[...] = acc0_ref[...]
        sweep(safe=True)

    @pl.when(kb == pl.num_programs(3) - 1)
    def _():
        lp = lp_ref[...]
        lam = (jnp.exp(jnp.sum(lp[0:1, :] * lp[1:2, :], axis=-1, keepdims=True))
               - jnp.exp(jnp.sum(lp[2:3, :] * lp[3:4, :], axis=-1, keepdims=True)) + lambda_init)
        o = (acc_ref[0:t, :] / jnp.tile(l_ref[0], (1, 2))
             - lam * (acc_ref[t:2 * t, :] / jnp.tile(l_ref[1], (1, 2))))
        ms = jnp.mean(o * o, axis=-1, keepdims=True)
        o_ref[...] = (o * lax.rsqrt(ms + EPS) * sw_ref[...] * (1.0 - lambda_init)).astype(o_ref.dtype)


def diff_attention(qn, kn, vb, bias_tiles, lam_params, subln_w, lambda_init):
    B, H, S, W = qn.shape
    t = bias_tiles.shape[-1]
    tk = min(ATTN_KEYS_PER_STEP, S)
    stat = pltpu.VMEM((2, t, LANES), jnp.float32)
    accs = pltpu.VMEM((2 * t, W), jnp.float32)
    return pl.pallas_call(
        functools.partial(_attn_kernel, lambda_init=lambda_init),
        grid=(B, H, S // t, S // tk),
        in_specs=[
            pl.BlockSpec((None, None, t, W), lambda b, h, i, j: (b, h, i, 0)),
            pl.BlockSpec((None, None, tk, W), lambda b, h, i, j: (b, h, j, 0)),
            pl.BlockSpec((None, None, tk, W), lambda b, h, i, j: (b, h, j, 0)),
            pl.BlockSpec((None, BIAS_TILES, t, t), lambda b, h, i, j: (h, 0, 0, 0)),
            pl.BlockSpec((4, DB_HEAD), lambda b, h, i, j: (0, 0)),
            pl.BlockSpec((1, W), lambda b, h, i, j: (0, 0)),
        ],
        out_specs=pl.BlockSpec((None, t, W), lambda b, h, i, j: (b, i, h)),
        out_shape=jax.ShapeDtypeStruct((B, S, DB_WIDTH), jnp.bfloat16),
        scratch_shapes=[stat, stat, accs, stat, stat, accs,
                        pltpu.VMEM((2, t, t), jnp.float32), pltpu.VMEM((2, t, t), jnp.float32),
                        pltpu.VMEM((2 * t, t), jnp.bfloat16), pltpu.VMEM((2 * t, t), jnp.bfloat16),
                        pltpu.VMEM((2 * t, LANES), jnp.float32), pltpu.VMEM((2 * t, LANES), jnp.float32)],
        compiler_params=_cparams(4),
        name="diff_attention",
    )(qn, kn, vb, bias_tiles, lam_params, subln_w.reshape(1, W))


ATTN_TILE = 512
ATTN_KEYS_PER_STEP = 2048


def _trunk(x, mod, layers, weights, bias_tiles):
    for l, p in enumerate(layers):
        m = mod[l]
        lambda_init = 0.8 - 0.6 * math.exp(-0.3 * l)
        h = norm_modulate(x, p["norm_attn_w"], m, 0, 1)
        zg = matmul_grouped(h, weights["w_in"], l, jnp.float32, tm=1024, tn=1024)
        o_fwd = hgrn_pass(zg, p["gates_fwd"], reverse=False)
        out_a = hgrn_pass(zg, p["gates_bwd"], reverse=True, o_fwd=o_fwd, norm_w=p["hgrn_norm_w"])
        qn, kn, vb = qk_norm(zg, p["q_norm_w"], p["k_norm_w"])
        out_b = diff_attention(qn, kn, vb, bias_tiles, p["diff_lambda"], p["diff_subln_w"], lambda_init)
        x = matmul_residual([out_a, out_b], weights["w_out"], l, x, m, 2, tm=1024, tn=1024)
        h = norm_modulate(x, p["norm_ffn_w"], m, 3, 4)
        act = up_conv_gate(h, weights["w_up"], l, p["conv_w"], p["conv_b"])
        x = matmul_residual([act], weights["w_down"], l, x, m, 5, tm=512, tn=512)
    return x


def kernel(x_prompt, x_sample, c_prompt, c_sample, w_ada, b_ada, norm_attn_w, w_in, hgrn_lb, hgrn_norm_w,
           q_norm_w, k_norm_w, diff_lambda, diff_subln_w, rel_bias, w_out, norm_ffn_w, w_up, conv_w, conv_b,
           w_down):
    L = w_ada.shape[0]
    D = x_prompt.shape[-1]
    Bp, Bs = c_prompt.shape[0], c_sample.shape[0]
    rows = -(-(Bp + Bs) // 8) * 8
    c_all = jnp.zeros((rows, D), jnp.float32).at[:Bp].set(c_prompt).at[Bp:Bp + Bs].set(c_sample)
    mod = adaln_mod(c_all, w_ada, b_ada)
    mod_p = mod[:, :Bp].reshape(L, Bp, 6, 1, D)
    mod_s = mod[:, Bp:Bp + Bs].reshape(L, Bs, 6, 1, D)

    lb_cum = jnp.cumsum(jax.nn.softmax(hgrn_lb.astype(jnp.float32), axis=0), axis=0)
    lb_all = lb_cum - lb_cum[0:1]
    log_lb, log_1m_lb, one_m_lb = jnp.log(lb_all), jnp.log1p(-lb_all), 1.0 - lb_all

    layers = []
    for l in range(L):
        def gates(d):
            return (log_lb[l, d][None], log_1m_lb[l, d][None], one_m_lb[l, d][None])
        layers.append(dict(
            norm_attn_w=norm_attn_w[l], gates_fwd=gates(0), gates_bwd=gates(1), hgrn_norm_w=hgrn_norm_w[l],
            q_norm_w=q_norm_w[l], k_norm_w=k_norm_w[l], diff_lambda=diff_lambda[l].astype(jnp.float32),
            diff_subln_w=diff_subln_w[l], norm_ffn_w=norm_ffn_w[l], conv_w=conv_w[l], conv_b=conv_b[l]))
    weights = dict(w_in=w_in.astype(jnp.bfloat16), w_out=w_out.astype(jnp.bfloat16),
                   w_up=tile_columns(w_up, min(UP_TN, w_up.shape[2] // 2)).astype(jnp.bfloat16),
                   w_down=w_down.astype(jnp.bfloat16))
    rel_bias = rel_bias.astype(jnp.float32)
    tiles = {}
    outs = []
    for x, m in ((x_prompt, mod_p), (x_sample, mod_s)):
        S = x.shape[1]
        tile = min(ATTN_TILE, S)
        if tile not in tiles:
            tiles[tile] = rel_bias_tiles(rel_bias, tile)
        outs.append(_trunk(x, m, layers, weights, tiles[tile]))
    return tuple(outs)
```

```python
import functools
import math

import jax
import jax.numpy as jnp
from jax import lax
from jax.experimental import pallas as pl
from jax.experimental.pallas import tpu as pltpu

D_MODEL = 4096
HA_HEADS = 16
HA_HEAD = 128
HA_WIDTH = HA_HEADS * HA_HEAD
DB_HEADS = 8
DB_HEAD = 128
DB_WIDTH = DB_HEADS * 2 * DB_HEAD
IN_COLS = 5 * HA_WIDTH + 3 * DB_WIDTH
CHUNK = 64
NUM_BUCKETS = 32
EPS = 1e-6
BUCKET_START = (0, 1, 2, 3, 4, 5, 6, 7, 8, 12, 16, 23, 32, 46, 64, 91)

LANES = 128
V7X_VMEM_BYTES = 64 * 1024 * 1024
VMEM_LIMIT = V7X_VMEM_BYTES - 8 * 1024 * 1024

LOG2E = 1.0 / math.log(2.0)
SAFE_SPAN = 80.0
SAFE_LOG2_GAP = 100.0

_NT = (((1,), (1,)), ((), ()))
_TN = (((0,), (0,)), ((), ()))


def _cparams(n_axes):
    return pltpu.CompilerParams(
        dimension_semantics=("arbitrary",) * n_axes, vmem_limit_bytes=VMEM_LIMIT
    )


def _silu(x):
    return x * jax.nn.sigmoid(x)


def _ada_kernel(c_ref, w_ref, b_ref, o_ref):
    @pl.when(pl.program_id(2) == 0)
    def _():
        o_ref[...] = jnp.broadcast_to(b_ref[...], o_ref.shape)

    a = _silu(c_ref[...]).astype(jnp.bfloat16)
    o_ref[...] += jnp.dot(a, w_ref[...].astype(jnp.bfloat16), preferred_element_type=jnp.float32)


def adaln_mod(c_all, w_ada, b_ada, *, tk=1024, tn=2048):
    R, D = c_all.shape
    L, _, N = w_ada.shape
    return pl.pallas_call(
        _ada_kernel,
        grid=(L, N // tn, D // tk),
        in_specs=[
            pl.BlockSpec((R, tk), lambda l, j, k: (0, k)),
            pl.BlockSpec((None, tk, tn), lambda l, j, k: (l, k, j)),
            pl.BlockSpec((None, 1, tn), lambda l, j, k: (l, 0, j)),
        ],
        out_specs=pl.BlockSpec((None, R, tn), lambda l, j, k: (l, 0, j)),
        out_shape=jax.ShapeDtypeStruct((L, R, N), jnp.float32),
        compiler_params=_cparams(3),
        name="adaln_mod",
    )(c_all, w_ada, b_ada.reshape(L, 1, N))


def _normmod_kernel(x_ref, w_ref, sc_ref, sh_ref, o_ref):
    x = x_ref[...]
    ms = jnp.mean(x * x, axis=-1, keepdims=True)
    y = x * lax.rsqrt(ms + EPS) * w_ref[...]
    o_ref[...] = (y * (1.0 + sc_ref[...]) + sh_ref[...]).astype(o_ref.dtype)


def norm_modulate(x, w, mod, shift_idx, scale_idx, *, ts=512):
    B, S, D = x.shape
    ts = min(ts, S)
    return pl.pallas_call(
        _normmod_kernel,
        grid=(B, S // ts),
        in_specs=[
            pl.BlockSpec((None, ts, D), lambda b, i: (b, i, 0)),
            pl.BlockSpec((1, D), lambda b, i: (0, 0)),
            pl.BlockSpec((None, None, 1, D), lambda b, i: (b, scale_idx, 0, 0)),
            pl.BlockSpec((None, None, 1, D), lambda b, i: (b, shift_idx, 0, 0)),
        ],
        out_specs=pl.BlockSpec((None, ts, D), lambda b, i: (b, i, 0)),
        out_shape=jax.ShapeDtypeStruct((B, S, D), jnp.bfloat16),
        compiler_params=_cparams(2),
        name="norm_modulate",
    )(x, w.reshape(1, D), mod, mod)


def _mm_grouped_kernel(a_ref, w_ref, o_ref):
    acc = jnp.dot(a_ref[...], w_ref[...], preferred_element_type=jnp.float32)
    for g in range(o_ref.shape[0]):
        o_ref[g] = acc[:, g * LANES:(g + 1) * LANES].astype(o_ref.dtype)


def matmul_grouped(a, w, layer, out_dtype, *, tm, tn):
    B, S, K = a.shape
    N = w.shape[2]
    tm, tn = min(tm, S), min(tn, N)
    return pl.pallas_call(
        _mm_grouped_kernel,
        grid=(B, S // tm, N // tn),
        in_specs=[
            pl.BlockSpec((None, tm, K), lambda b, i, j: (b, i, 0)),
            pl.BlockSpec((None, K, tn), lambda b, i, j: (layer, 0, j)),
        ],
        out_specs=pl.BlockSpec((None, tn // LANES, tm, LANES), lambda b, i, j: (b, j, i, 0)),
        out_shape=jax.ShapeDtypeStruct((B, N // LANES, S, LANES), out_dtype),
        compiler_params=_cparams(3),
        name="matmul_grouped",
    )(a, w)


def _mm_res_kernel(*refs, n_lhs):
    a_refs = refs[:n_lhs]
    w_ref, x_ref, g_ref, o_ref = refs[n_lhs:]
    acc = None
    k0 = 0
    for a_ref in a_refs:
        kw = a_ref.shape[-1]
        part = jnp.dot(a_ref[...], w_ref[k0:k0 + kw, :], preferred_element_type=jnp.float32)
        acc = part if acc is None else acc + part
        k0 += kw
    o_ref[...] = x_ref[...] + g_ref[...] * acc


def matmul_residual(lhs_parts, w, layer, x, mod, gate_idx, *, tm, tn):
    B, S, N = x.shape
    K = w.shape[1]
    tm, tn = min(tm, S), min(tn, N)
    n_lhs = len(lhs_parts)
    in_specs = [pl.BlockSpec((None, tm, a.shape[-1]), lambda b, i, j: (b, i, 0)) for a in lhs_parts]
    in_specs += [
        pl.BlockSpec((None, K, tn), lambda b, i, j: (layer, 0, j)),
        pl.BlockSpec((None, tm, tn), lambda b, i, j: (b, i, j)),
        pl.BlockSpec((None, None, 1, tn), lambda b, i, j: (b, gate_idx, 0, j)),
    ]
    return pl.pallas_call(
        functools.partial(_mm_res_kernel, n_lhs=n_lhs),
        grid=(B, S // tm, N // tn),
        in_specs=in_specs,
        out_specs=pl.BlockSpec((None, tm, tn), lambda b, i, j: (b, i, j)),
        out_shape=jax.ShapeDtypeStruct((B, S, N), jnp.float32),
        compiler_params=_cparams(3),
        name="matmul_residual",
    )(*lhs_parts, w, x, mod)


CONV_ROWS = 64


def _convgate_kernel(uv_ref, uvp_ref, uvn_ref, ug_ref, ugp_ref, ugn_ref, wv_ref, wg_ref, bv_ref, bg_ref, o_ref):
    i = pl.program_id(1)
    last = pl.num_programs(1) - 1
    gc, ts, _ = uv_ref.shape
    rc = min(CONV_ROWS, ts)
    n_chunk = ts // rc
    row = lax.broadcasted_iota(jnp.int32, (rc, LANES), 0)

    def conv(g, c, cur_ref, prev_ref, next_ref, w_ref, b_ref):
        lanes = slice(g * LANES, (g + 1) * LANES)
        r0 = c * rc
        cur = cur_ref[g, r0:r0 + rc, :]
        if c == 0:
            prow = jnp.where(i > 0, prev_ref[g][7:8, :], 0.0)
            up = jnp.where(row == 0, prow, pltpu.roll(cur, 1, 0))
        else:
            up = cur_ref[g, r0 - 1:r0 - 1 + rc, :]
        if c == n_chunk - 1:
            nrow = jnp.where(i < last, next_ref[g][0:1, :], 0.0)
            dn = jnp.where(row == rc - 1, nrow, pltpu.roll(cur, rc - 1, 0))
        else:
            dn = cur_ref[g, r0 + 1:r0 + 1 + rc, :]
        return up * w_ref[0:1, lanes] + cur * w_ref[1:2, lanes] + dn * w_ref[2:3, lanes] + b_ref[:, lanes]

    for g in range(gc):
        for c in range(n_chunk):
            val = conv(g, c, uv_ref, uvp_ref, uvn_ref, wv_ref, bv_ref)
            gate = conv(g, c, ug_ref, ugp_ref, ugn_ref, wg_ref, bg_ref)
            o_ref[c * rc:(c + 1) * rc, g * LANES:(g + 1) * LANES] = (_silu(gate) * val).astype(o_ref.dtype)


def conv_gate(u, conv_w, conv_b, *, ts=2048, gc=2):
    B, G2, S, _ = u.shape
    nf = G2 // 2
    F = nf * LANES
    ts = min(ts, S)
    nj = nf // gc
    r8 = ts // 8

    def cur(off):
        return pl.BlockSpec((None, gc, ts, LANES), lambda b, i, j: (b, off + j, i, 0))

    def prev(off):
        return pl.BlockSpec((None, gc, 8, LANES), lambda b, i, j: (b, off + j, jnp.maximum(i * r8 - 1, 0), 0))

    def nxt(off):
        return pl.BlockSpec((None, gc, 8, LANES),
                            lambda b, i, j: (b, off + j, jnp.minimum((i + 1) * r8, S // 8 - 1), 0))

    def par(rows, off):
        return pl.BlockSpec((rows, gc * LANES), lambda b, i, j: (0, off + j))

    cb = conv_b.reshape(1, 2 * F)
    return pl.pallas_call(
        _convgate_kernel,
        grid=(B, S // ts, nj),
        in_specs=[cur(0), prev(0), nxt(0), cur(nj), prev(nj), nxt(nj), par(3, 0), par(3, nj), par(1, 0), par(1, nj)],
        out_specs=pl.BlockSpec((None, ts, gc * LANES), lambda b, i, j: (b, i, j)),
        out_shape=jax.ShapeDtypeStruct((B, S, F), jnp.bfloat16),
        compiler_params=_cparams(3),
        name="conv_gate",
    )(u, u, u, u, u, u, conv_w, conv_w, cb, cb)


def _hgrn_kernel(*refs, reverse, T):
    if reverse:
        (xq_ref, xf_ref, xi_ref, xg_ref, of_ref, llb_ref, l1m_ref, oml_ref, nw_ref, o_ref,
         st_ref, oin_ref, q_ref, k_ref, b_ref) = refs
    else:
        xq_ref, xf_ref, xi_ref, llb_ref, l1m_ref, oml_ref, o_ref, st_ref, oin_ref, q_ref, k_ref, b_ref = refs
    C = CHUNK
    nC = T // C
    half = C // 2
    quarter = C // 4

    @pl.when(pl.program_id(2) == 0)
    def _():
        st_ref[...] = jnp.zeros_like(st_ref)

    xf = xf_ref[...]
    llb, l1m, oml = llb_ref[...], l1m_ref[...], oml_ref[...]
    q = _silu(xq_ref[...])
    v16 = xi_ref[...].astype(jnp.bfloat16)
    e = jnp.exp(-jnp.abs(xf))
    y = l1m + jnp.minimum(xf, 0.0) - jnp.log(1.0 + e)
    logf = jnp.maximum(llb, y) + jnp.log(1.0 + jnp.exp(-jnp.abs(llb - y)))
    kk = oml * jnp.where(xf >= 0.0, e, 1.0) / (1.0 + e)
    span = jnp.sum((-logf).reshape(T // quarter, quarter, LANES), axis=1)
    exact_needed = jnp.max(span) > SAFE_SPAN

    rc = lax.broadcasted_iota(jnp.int32, (T, LANES), 0) & (C - 1)
    b = logf
    for sft in (1, 2, 4, 8, 16, 32):
        if reverse:
            b = b + jnp.where(rc < C - sft, pltpu.roll(b, T - sft, 0), 0.0)
        else:
            b = b + jnp.where(rc >= sft, pltpu.roll(b, sft, 0), 0.0)

    def rows(x, c):
        return x[c * C:(c + 1) * C, :]

    def b_last(bc):
        return bc[0:1, :] if reverse else bc[C - 1:C, :]

    ti = lax.broadcasted_iota(jnp.int32, (C, C), 0)
    si = lax.broadcasted_iota(jnp.int32, (C, C), 1)
    causal = (si >= ti) if reverse else (si <= ti)

    first = lax.broadcasted_iota(jnp.int32, (C, LANES), 0) < half
    q_side = first if reverse else jnp.logical_not(first)
    m0, m1, bnd = (quarter, half + quarter, half) if reverse else (quarter - 1, half + quarter - 1, half - 1)
    zero = jnp.zeros((C, LANES), jnp.bfloat16)

    def intra_factored(c, bc):
        mid = jnp.where(first, bc[m0:m0 + 1, :], bc[m1:m1 + 1, :])
        qa = (rows(q, c) * jnp.exp(bc - mid)).astype(jnp.bfloat16)
        ka = (rows(kk, c) * jnp.exp(mid - bc)).astype(jnp.bfloat16)
        eb = jnp.exp(-jnp.abs(bc - bc[bnd:bnd + 1, :]))
        qb = (rows(q, c) * eb).astype(jnp.bfloat16)
        kb = (rows(kk, c) * eb).astype(jnp.bfloat16)
        qcat = jnp.concatenate([jnp.where(first, qa, zero), jnp.where(first, zero, qa),
                                jnp.where(q_side, qb, zero)], axis=1)
        kcat = jnp.concatenate([jnp.where(first, ka, zero), jnp.where(first, zero, ka),
                                jnp.where(q_side, zero, kb)], axis=1)
        s = lax.dot_general(qcat, kcat, _NT, preferred_element_type=jnp.float32)
        s = jnp.where(causal, s, 0.0).astype(jnp.bfloat16)
        return jnp.dot(s, rows(v16, c), preferred_element_type=jnp.float32)

    def emit(rsl, o):
        if reverse:
            tot = of_ref[rsl, :] + o
            ms = jnp.mean(tot * tot, axis=-1, keepdims=True)
            yn = tot * lax.rsqrt(ms + EPS) * nw_ref[...]
            o_ref[rsl, :] = (yn * _silu(xg_ref[rsl, :])).astype(o_ref.dtype)
        else:
            o_ref[rsl, :] = o

    st = st_ref[...]
    for c in (range(nC - 1, -1, -1) if reverse else range(nC)):
        rsl = slice(c * C, (c + 1) * C)
        bc = rows(b, c)
        bl = b_last(bc)
        qa = (rows(q, c) * jnp.exp(bc)).astype(jnp.bfloat16)
        o_inter = lax.dot_general(qa, st.astype(jnp.bfloat16), _NT, preferred_element_type=jnp.float32)
        oin_ref[rsl, :] = o_inter
        emit(rsl, intra_factored(c, bc) + o_inter)
        kl = (rows(kk, c) * jnp.exp(bl - bc)).astype(jnp.bfloat16)
        st = st * jnp.exp(bl) + lax.dot_general(rows(v16, c), kl, _TN, preferred_element_type=jnp.float32)
    st_ref[...] = st

    @pl.when(exact_needed)
    def _():
        q_ref[...] = q
        k_ref[...] = kk
        b_ref[...] = b
        row = lax.broadcasted_iota(jnp.int32, (C, LANES), 0)

        def per_chunk(c, carry):
            rsl = pl.ds(pl.multiple_of(c * C, C), C)
            qc = q_ref[rsl, :]
            kc = k_ref[rsl, :]
            bc = b_ref[rsl, :]
            vc = xi_ref[rsl, :]

            def pair(d, acc):
                sh = ((C - d) % C) if reverse else d
                valid = (row < C - d) if reverse else (row >= d)
                dec = jnp.exp(jnp.where(valid, bc - pltpu.roll(bc, sh, 0), -jnp.inf))
                w = jnp.sum(qc * pltpu.roll(kc, sh, 0) * dec, axis=-1, keepdims=True)
                return acc + w * pltpu.roll(vc, sh, 0)

            oi = lax.fori_loop(0, C, pair, jnp.zeros((C, LANES), jnp.float32))
            emit(rsl, oi + oin_ref[rsl, :])
            return carry

        lax.fori_loop(0, nC, per_chunk, 0)


def hgrn_pass(zg, gate_params, *, reverse, o_fwd=None, norm_w=None, T=512):
    B, _, S, _ = zg.shape
    T = min(T, S)
    nblk = S // T
    H = HA_HEADS

    def seq(c):
        return (nblk - 1 - c) if reverse else c

    def zspec(section):
        return pl.BlockSpec((None, None, T, LANES), lambda b, h, c: (b, section * H + h, seq(c), 0))

    hspec = pl.BlockSpec((None, None, T, LANES), lambda b, h, c: (b, h, seq(c), 0))
    pspec = pl.BlockSpec((1, LANES), lambda b, h, c: (0, h))
    llb, l1m, oml = gate_params
    if reverse:
        in_specs = [zspec(0), zspec(2), zspec(3), zspec(4), hspec,
                    pspec, pspec, pspec, pl.BlockSpec((1, LANES), lambda b, h, c: (0, 0))]
        args = (zg, zg, zg, zg, o_fwd, llb, l1m, oml, norm_w.reshape(1, LANES))
        out_spec = pl.BlockSpec((None, T, LANES), lambda b, h, c: (b, seq(c), h))
        out_shape = jax.ShapeDtypeStruct((B, S, HA_WIDTH), jnp.bfloat16)
    else:
        in_specs = [zspec(0), zspec(1), zspec(3), pspec, pspec, pspec]
        args = (zg, zg, zg, llb, l1m, oml)
        out_spec = hspec
        out_shape = jax.ShapeDtypeStruct((B, H, S, LANES), jnp.float32)
    blk = pltpu.VMEM((T, LANES), jnp.float32)
    return pl.pallas_call(
        functools.partial(_hgrn_kernel, reverse=reverse, T=T),
        grid=(B, H, nblk),
        in_specs=in_specs,
        out_specs=out_spec,
        out_shape=out_shape,
        scratch_shapes=[pltpu.VMEM((HA_HEAD, HA_HEAD), jnp.float32), blk, blk, blk, blk],
        compiler_params=_cparams(3),
        name="hgrn_bwd" if reverse else "hgrn_fwd",
    )(*args)


def _qknorm_kernel(q_ref, k_ref, v_ref, qw_ref, kw_ref, qo_ref, ko_ref, vo_ref, *, q_scale):
    qw = qw_ref[...] * q_scale
    kw = kw_ref[...]
    for g in range(q_ref.shape[0]):
        h, c = divmod(g, 2)
        sl = slice(c * DB_HEAD, (c + 1) * DB_HEAD)
        for x_ref, w, o_ref in ((q_ref, qw, qo_ref), (k_ref, kw, ko_ref)):
            x = x_ref[g]
            ms = jnp.mean(x * x, axis=-1, keepdims=True)
            o_ref[h, :, sl] = (x * lax.rsqrt(ms + EPS) * w).astype(o_ref.dtype)
        vo_ref[h, :, sl] = v_ref[g].astype(vo_ref.dtype)


def qk_norm(zg, q_norm_w, k_norm_w, *, ts=512):
    B, _, S, _ = zg.shape
    ts = min(ts, S)
    ng = DB_WIDTH // LANES
    first = 5 * HA_WIDTH // DB_WIDTH

    def zspec(sec):
        return pl.BlockSpec((None, ng, ts, LANES), lambda b, i: (b, first + sec, i, 0))

    wspec = pl.BlockSpec((1, DB_HEAD), lambda b, i: (0, 0))
    ospec = pl.BlockSpec((None, DB_HEADS, ts, 2 * DB_HEAD), lambda b, i: (b, 0, i, 0))
    oshape = jax.ShapeDtypeStruct((B, DB_HEADS, S, 2 * DB_HEAD), jnp.bfloat16)
    return pl.pallas_call(
        functools.partial(_qknorm_kernel, q_scale=DB_HEAD ** -0.5 * LOG2E),
        grid=(B, S // ts),
        in_specs=[zspec(0), zspec(1), zspec(2), wspec, wspec],
        out_specs=(ospec, ospec, ospec),
        out_shape=(oshape, oshape, oshape),
        compiler_params=_cparams(2),
        name="qk_norm",
    )(zg, zg, zg, q_norm_w.reshape(1, DB_HEAD), k_norm_w.reshape(1, DB_HEAD))


BIAS_TILES = 5


def _bias_kernel(rb_ref, o_ref, *, t):
    h = pl.program_id(0)
    offset = (pl.program_id(1) - BIAS_TILES // 2) * t
    qi = lax.broadcasted_iota(jnp.int32, (t, t), 0)
    kj = lax.broadcasted_iota(jnp.int32, (t, t), 1)
    rel = kj - qi + offset
    n = jnp.abs(rel)
    half = NUM_BUCKETS // 2

    def side(base):
        val = jnp.full((t, t), rb_ref[base + half - 1, h], jnp.float32)
        for c in range(half - 2, -1, -1):
            val = jnp.where(n < BUCKET_START[c + 1], rb_ref[base + c, h], val)
        return val

    o_ref[...] = jnp.where(rel > 0, side(half), side(0)) * LOG2E


def rel_bias_tiles(rel_bias, t):
    assert t >= 128
    return pl.pallas_call(
        functools.partial(_bias_kernel, t=t),
        grid=(DB_HEADS, BIAS_TILES),
        in_specs=[pl.BlockSpec(memory_space=pltpu.SMEM)],
        out_specs=pl.BlockSpec((None, None, t, t), lambda h, d: (h, d, 0, 0)),
        out_shape=jax.ShapeDtypeStruct((DB_HEADS, BIAS_TILES, t, t), jnp.float32),
        compiler_params=_cparams(2),
        name="rel_bias_tiles",
    )(rel_bias)


ATTN_ROWS = 32


def _attn_kernel(q_ref, k_ref, v_ref, bias_ref, lp_ref, sw_ref, o_ref,
                 m_ref, l_ref, acc_ref, m0_ref, l0_ref, acc0_ref, s_a, s_b, p_a, p_b, al_a, al_b, *, lambda_init):
    qb = pl.program_id(2)
    kb = pl.program_id(3)
    t = q_ref.shape[0]
    n_sub = k_ref.shape[0] // t
    rc = min(ATTN_ROWS, t)
    s_bufs, p_bufs, al_bufs = (s_a, s_b), (p_a, p_b), (al_a, al_b)

    @pl.when(kb == 0)
    def _():
        m_ref[...] = jnp.zeros_like(m_ref)
        l_ref[...] = jnp.zeros_like(l_ref)
        acc_ref[...] = jnp.zeros_like(acc_ref)

    m0_ref[...] = m_ref[...]
    l0_ref[...] = l_ref[...]
    acc0_ref[...] = acc_ref[...]

    def logits_into(j):
        for c in range(2):
            s_bufs[j % 2][c] = lax.dot_general(q_ref[:, c * DB_HEAD:(c + 1) * DB_HEAD],
                                               k_ref[j * t:(j + 1) * t, c * DB_HEAD:(c + 1) * DB_HEAD],
                                               _NT, preferred_element_type=jnp.float32)

    def widen(x, width):
        return jnp.tile(x, (1, width // LANES))

    def sweep(safe):
        excess = jnp.zeros((rc, LANES), jnp.float32)
        logits_into(0)
        for j in range(n_sub):
            if j + 1 < n_sub:
                logits_into(j + 1)
            s_buf, p_buf, al_buf = s_bufs[j % 2], p_bufs[j % 2], al_bufs[j % 2]
            tile_idx = jnp.clip(kb * n_sub + j - qb, -2, 2) + 2
            for i in range(t // rc):
                rows = slice(i * rc, (i + 1) * rc)
                bias = bias_ref[tile_idx, rows, :]
                for c in range(2):
                    crows = slice(c * t + i * rc, c * t + (i + 1) * rc)
                    s = s_buf[c, rows, :] + bias
                    smax = jnp.max(s, axis=-1, keepdims=True)
                    r = m_ref[c, rows, :]
                    m_new = jnp.maximum(r, smax)
                    alpha = jnp.exp2(r - m_new)
                    if safe:
                        p = jnp.exp2(s - widen(m_new, t))
                        l_ref[c, rows, :] = alpha * l_ref[c, rows, :] + jnp.sum(p, axis=-1, keepdims=True)
                    else:
                        p = jnp.exp2(s - widen(r, t))
                        l_ref[c, rows, :] = alpha * (l_ref[c, rows, :] + jnp.sum(p, axis=-1, keepdims=True))
                        gap = smax - r
                        if j == 0:
                            gap = jnp.where(kb == 0, jnp.abs(gap), gap)
                        excess = jnp.maximum(excess, gap)
                    m_ref[c, rows, :] = m_new
                    p_buf[crows, :] = p.astype(jnp.bfloat16)
                    al_buf[crows, :] = alpha
            pv = jnp.dot(p_buf[...], v_ref[j * t:(j + 1) * t, :], preferred_element_type=jnp.float32)
            a2 = widen(al_buf[...], 2 * LANES)
            acc_ref[...] = (a2 * acc_ref[...] + pv) if safe else (a2 * (acc_ref[...] + pv))
        return excess

    redo = jnp.max(sweep(safe=False)) > SAFE_LOG2_GAP

    @pl.when(redo)
    def _():
        m_ref[...] = m0_ref[...] + jnp.where(kb == 0, -jnp.inf, 0.0)
        l_ref[...] = l0_ref[...]
        acc_ref[...] = acc0_ref[...]
        sweep(safe=True)

    @pl.when(kb == pl.num_programs(3) - 1)
    def _():
        lp = lp_ref[...]
        lam = (jnp.exp(jnp.sum(lp[0:1, :] * lp[1:2, :], axis=-1, keepdims=True))
               - jnp.exp(jnp.sum(lp[2:3, :] * lp[3:4, :], axis=-1, keepdims=True)) + lambda_init)
        o = (acc_ref[0:t, :] / jnp.tile(l_ref[0], (1, 2))
             - lam * (acc_ref[t:2 * t, :] / jnp.tile(l_ref[1], (1, 2))))
        ms = jnp.mean(o * o, axis=-1, keepdims=True)
        o_ref[...] = (o * lax.rsqrt(ms + EPS) * sw_ref[...] * (1.0 - lambda_init)).astype(o_ref.dtype)


def diff_attention(qn, kn, vb, bias_tiles, lam_params, subln_w, lambda_init):
    B, H, S, W = qn.shape
    t = bias_tiles.shape[-1]
    tk = min(ATTN_KEYS_PER_STEP, S)
    stat = pltpu.VMEM((2, t, LANES), jnp.float32)
    accs = pltpu.VMEM((2 * t, W), jnp.float32)
    return pl.pallas_call(
        functools.partial(_attn_kernel, lambda_init=lambda_init),
        grid=(B, H, S // t, S // tk),
        in_specs=[
            pl.BlockSpec((None, None, t, W), lambda b, h, i, j: (b, h, i, 0)),
            pl.BlockSpec((None, None, tk, W), lambda b, h, i, j: (b, h, j, 0)),
            pl.BlockSpec((None, None, tk, W), lambda b, h, i, j: (b, h, j, 0)),
            pl.BlockSpec((None, BIAS_TILES, t, t), lambda b, h, i, j: (h, 0, 0, 0)),
            pl.BlockSpec((4, DB_HEAD), lambda b, h, i, j: (0, 0)),
            pl.BlockSpec((1, W), lambda b, h, i, j: (0, 0)),
        ],
        out_specs=pl.BlockSpec((None, t, W), lambda b, h, i, j: (b, i, h)),
        out_shape=jax.ShapeDtypeStruct((B, S, DB_WIDTH), jnp.bfloat16),
        scratch_shapes=[stat, stat, accs, stat, stat, accs,
                        pltpu.VMEM((2, t, t), jnp.float32), pltpu.VMEM((2, t, t), jnp.float32),
                        pltpu.VMEM((2 * t, t), jnp.bfloat16), pltpu.VMEM((2 * t, t), jnp.bfloat16),
                        pltpu.VMEM((2 * t, LANES), jnp.float32), pltpu.VMEM((2 * t, LANES), jnp.float32)],
        compiler_params=_cparams(4),
        name="diff_attention",
    )(qn, kn, vb, bias_tiles, lam_params, subln_w.reshape(1, W))


ATTN_TILE = 512
ATTN_KEYS_PER_STEP = 2048


def _trunk(x, mod, layers, weights, bias_tiles):
    for l, p in enumerate(layers):
        m = mod[l]
        lambda_init = 0.8 - 0.6 * math.exp(-0.3 * l)
        h = norm_modulate(x, p["norm_attn_w"], m, 0, 1)
        zg = matmul_grouped(h, weights["w_in"], l, jnp.float32, tm=1024, tn=1024)
        o_fwd = hgrn_pass(zg, p["gates_fwd"], reverse=False)
        out_a = hgrn_pass(zg, p["gates_bwd"], reverse=True, o_fwd=o_fwd, norm_w=p["hgrn_norm_w"])
        qn, kn, vb = qk_norm(zg, p["q_norm_w"], p["k_norm_w"])
        out_b = diff_attention(qn, kn, vb, bias_tiles, p["diff_lambda"], p["diff_subln_w"], lambda_init)
        x = matmul_residual([out_a, out_b], weights["w_out"], l, x, m, 2, tm=1024, tn=1024)
        h = norm_modulate(x, p["norm_ffn_w"], m, 3, 4)
        ug = matmul_grouped(h, weights["w_up"], l, jnp.float32, tm=2048, tn=512)
        act = conv_gate(ug, p["conv_w"], p["conv_b"])
        x = matmul_residual([act], weights["w_down"], l, x, m, 5, tm=512, tn=512)
    return x


def kernel(x_prompt, x_sample, c_prompt, c_sample, w_ada, b_ada, norm_attn_w, w_in, hgrn_lb, hgrn_norm_w,
           q_norm_w, k_norm_w, diff_lambda, diff_subln_w, rel_bias, w_out, norm_ffn_w, w_up, conv_w, conv_b,
           w_down):
    L = w_ada.shape[0]
    D = x_prompt.shape[-1]
    Bp, Bs = c_prompt.shape[0], c_sample.shape[0]
    rows = -(-(Bp + Bs) // 8) * 8
    c_all = jnp.zeros((rows, D), jnp.float32).at[:Bp].set(c_prompt).at[Bp:Bp + Bs].set(c_sample)
    mod = adaln_mod(c_all, w_ada, b_ada)
    mod_p = mod[:, :Bp].reshape(L, Bp, 6, 1, D)
    mod_s = mod[:, Bp:Bp + Bs].reshape(L, Bs, 6, 1, D)

    lb_cum = jnp.cumsum(jax.nn.softmax(hgrn_lb.astype(jnp.float32), axis=0), axis=0)
    lb_all = lb_cum - lb_cum[0:1]
    log_lb, log_1m_lb, one_m_lb = jnp.log(lb_all), jnp.log1p(-lb_all), 1.0 - lb_all

    layers = []
    for l in range(L):
        def gates(d):
            return (log_lb[l, d][None], log_1m_lb[l, d][None], one_m_lb[l, d][None])
        layers.append(dict(
            norm_attn_w=norm_attn_w[l], gates_fwd=gates(0), gates_bwd=gates(1), hgrn_norm_w=hgrn_norm_w[l],
            q_norm_w=q_norm_w[l], k_norm_w=k_norm_w[l], diff_lambda=diff_lambda[l].astype(jnp.float32),
            diff_subln_w=diff_subln_w[l], norm_ffn_w=norm_ffn_w[l], conv_w=conv_w[l], conv_b=conv_b[l]))
    weights = dict(w_in=w_in.astype(jnp.bfloat16), w_out=w_out.astype(jnp.bfloat16),
                   w_up=w_up.astype(jnp.bfloat16), w_down=w_down.astype(jnp.bfloat16))
    rel_bias = rel_bias.astype(jnp.float32)
    tiles = {}
    outs = []
    for x, m in ((x_prompt, mod_p), (x_sample, mod_s)):
        S = x.shape[1]
        tile = min(ATTN_TILE, S)
        if tile not in tiles:
            tiles[tile] = rel_bias_tiles(rel_bias, tile)
        outs.append(_trunk(x, m, layers, weights, tiles[tile]))
    return tuple(outs)
```

```python
import functools
import math

import jax
import jax.numpy as jnp
from jax import lax
from jax.experimental import pallas as pl
from jax.experimental.pallas import tpu as pltpu

D_MODEL = 4096
HA_HEADS = 16
HA_HEAD = 128
HA_WIDTH = HA_HEADS * HA_HEAD
DB_HEADS = 8
DB_HEAD = 128
DB_WIDTH = DB_HEADS * 2 * DB_HEAD
IN_COLS = 5 * HA_WIDTH + 3 * DB_WIDTH
CHUNK = 64
NUM_BUCKETS = 32
EPS = 1e-6
BUCKET_START = (0, 1, 2, 3, 4, 5, 6, 7, 8, 12, 16, 23, 32, 46, 64, 91)

LANES = 128
V7X_VMEM_BYTES = 64 * 1024 * 1024
VMEM_LIMIT = V7X_VMEM_BYTES - 8 * 1024 * 1024

LOG2E = 1.0 / math.log(2.0)
SAFE_SPAN = 80.0
SAFE_LOG2_GAP = 100.0

_NT = (((1,), (1,)), ((), ()))
_TN = (((0,), (0,)), ((), ()))


def _cparams(n_axes):
    return pltpu.CompilerParams(
        dimension_semantics=("arbitrary",) * n_axes, vmem_limit_bytes=VMEM_LIMIT
    )


def _silu(x):
    return x * jax.nn.sigmoid(x)


def _ada_kernel(c_ref, w_ref, b_ref, o_ref):
    @pl.when(pl.program_id(2) == 0)
    def _():
        o_ref[...] = jnp.broadcast_to(b_ref[...], o_ref.shape)

    a = _silu(c_ref[...]).astype(jnp.bfloat16)
    o_ref[...] += jnp.dot(a, w_ref[...].astype(jnp.bfloat16), preferred_element_type=jnp.float32)


def adaln_mod(c_all, w_ada, b_ada, *, tk=1024, tn=2048):
    R, D = c_all.shape
    L, _, N = w_ada.shape
    return pl.pallas_call(
        _ada_kernel,
        grid=(L, N // tn, D // tk),
        in_specs=[
            pl.BlockSpec((R, tk), lambda l, j, k: (0, k)),
            pl.BlockSpec((None, tk, tn), lambda l, j, k: (l, k, j)),
            pl.BlockSpec((None, 1, tn), lambda l, j, k: (l, 0, j)),
        ],
        out_specs=pl.BlockSpec((None, R, tn), lambda l, j, k: (l, 0, j)),
        out_shape=jax.ShapeDtypeStruct((L, R, N), jnp.float32),
        compiler_params=_cparams(3),
        name="adaln_mod",
    )(c_all, w_ada, b_ada.reshape(L, 1, N))


def _normmod_kernel(x_ref, w_ref, sc_ref, sh_ref, o_ref):
    x = x_ref[...]
    ms = jnp.mean(x * x, axis=-1, keepdims=True)
    y = x * lax.rsqrt(ms + EPS) * w_ref[...]
    o_ref[...] = (y * (1.0 + sc_ref[...]) + sh_ref[...]).astype(o_ref.dtype)


def norm_modulate(x, w, mod, shift_idx, scale_idx, *, ts=512):
    B, S, D = x.shape
    ts = min(ts, S)
    return pl.pallas_call(
        _normmod_kernel,
        grid=(B, S // ts),
        in_specs=[
            pl.BlockSpec((None, ts, D), lambda b, i: (b, i, 0)),
            pl.BlockSpec((1, D), lambda b, i: (0, 0)),
            pl.BlockSpec((None, None, 1, D), lambda b, i: (b, scale_idx, 0, 0)),
            pl.BlockSpec((None, None, 1, D), lambda b, i: (b, shift_idx, 0, 0)),
        ],
        out_specs=pl.BlockSpec((None, ts, D), lambda b, i: (b, i, 0)),
        out_shape=jax.ShapeDtypeStruct((B, S, D), jnp.bfloat16),
        compiler_params=_cparams(2),
        name="norm_modulate",
    )(x, w.reshape(1, D), mod, mod)


def _mm_grouped_kernel(a_ref, w_ref, o_ref):
    acc = jnp.dot(a_ref[...], w_ref[...], preferred_element_type=jnp.float32)
    for g in range(o_ref.shape[0]):
        o_ref[g] = acc[:, g * LANES:(g + 1) * LANES].astype(o_ref.dtype)


def matmul_grouped(a, w, layer, out_dtype, *, tm, tn):
    B, S, K = a.shape
    N = w.shape[2]
    tm, tn = min(tm, S), min(tn, N)
    return pl.pallas_call(
        _mm_grouped_kernel,
        grid=(B, S // tm, N // tn),
        in_specs=[
            pl.BlockSpec((None, tm, K), lambda b, i, j: (b, i, 0)),
            pl.BlockSpec((None, K, tn), lambda b, i, j: (layer, 0, j)),
        ],
        out_specs=pl.BlockSpec((None, tn // LANES, tm, LANES), lambda b, i, j: (b, j, i, 0)),
        out_shape=jax.ShapeDtypeStruct((B, N // LANES, S, LANES), out_dtype),
        compiler_params=_cparams(3),
        name="matmul_grouped",
    )(a, w)


def _mm_res_kernel(*refs, n_lhs):
    a_refs = refs[:n_lhs]
    w_ref, x_ref, g_ref, o_ref = refs[n_lhs:]
    acc = None
    k0 = 0
    for a_ref in a_refs:
        kw = a_ref.shape[-1]
        part = jnp.dot(a_ref[...], w_ref[k0:k0 + kw, :], preferred_element_type=jnp.float32)
        acc = part if acc is None else acc + part
        k0 += kw
    o_ref[...] = x_ref[...] + g_ref[...] * acc


def matmul_residual(lhs_parts, w, layer, x, mod, gate_idx, *, tm, tn):
    B, S, N = x.shape
    K = w.shape[1]
    tm, tn = min(tm, S), min(tn, N)
    n_lhs = len(lhs_parts)
    in_specs = [pl.BlockSpec((None, tm, a.shape[-1]), lambda b, i, j: (b, i, 0)) for a in lhs_parts]
    in_specs += [
        pl.BlockSpec((None, K, tn), lambda b, i, j: (layer, 0, j)),
        pl.BlockSpec((None, tm, tn), lambda b, i, j: (b, i, j)),
        pl.BlockSpec((None, None, 1, tn), lambda b, i, j: (b, gate_idx, 0, j)),
    ]
    return pl.pallas_call(
        functools.partial(_mm_res_kernel, n_lhs=n_lhs),
        grid=(B, S // tm, N // tn),
        in_specs=in_specs,
        out_specs=pl.BlockSpec((None, tm, tn), lambda b, i, j: (b, i, j)),
        out_shape=jax.ShapeDtypeStruct((B, S, N), jnp.float32),
        compiler_params=_cparams(3),
        name="matmul_residual",
    )(*lhs_parts, w, x, mod)


CONV_ROWS = 64


def _convgate_kernel(uv_ref, uvp_ref, uvn_ref, ug_ref, ugp_ref, ugn_ref, wv_ref, wg_ref, bv_ref, bg_ref, o_ref):
    i = pl.program_id(1)
    last = pl.num_programs(1) - 1
    gc, ts, _ = uv_ref.shape
    rc = min(CONV_ROWS, ts)
    n_chunk = ts // rc
    row = lax.broadcasted_iota(jnp.int32, (rc, LANES), 0)

    def conv(g, c, cur_ref, prev_ref, next_ref, w_ref, b_ref):
        lanes = slice(g * LANES, (g + 1) * LANES)
        r0 = c * rc
        cur = cur_ref[g, r0:r0 + rc, :]
        if c == 0:
            prow = jnp.where(i > 0, prev_ref[g][7:8, :], 0.0)
            up = jnp.where(row == 0, prow, pltpu.roll(cur, 1, 0))
        else:
            up = cur_ref[g, r0 - 1:r0 - 1 + rc, :]
        if c == n_chunk - 1:
            nrow = jnp.where(i < last, next_ref[g][0:1, :], 0.0)
            dn = jnp.where(row == rc - 1, nrow, pltpu.roll(cur, rc - 1, 0))
        else:
            dn = cur_ref[g, r0 + 1:r0 + 1 + rc, :]
        return up * w_ref[0:1, lanes] + cur * w_ref[1:2, lanes] + dn * w_ref[2:3, lanes] + b_ref[:, lanes]

    for g in range(gc):
        for c in range(n_chunk):
            val = conv(g, c, uv_ref, uvp_ref, uvn_ref, wv_ref, bv_ref)
            gate = conv(g, c, ug_ref, ugp_ref, ugn_ref, wg_ref, bg_ref)
            o_ref[c * rc:(c + 1) * rc, g * LANES:(g + 1) * LANES] = (_silu(gate) * val).astype(o_ref.dtype)


def conv_gate(u, conv_w, conv_b, *, ts=2048, gc=2):
    B, G2, S, _ = u.shape
    nf = G2 // 2
    F = nf * LANES
    ts = min(ts, S)
    nj = nf // gc
    r8 = ts // 8

    def cur(off):
        return pl.BlockSpec((None, gc, ts, LANES), lambda b, i, j: (b, off + j, i, 0))

    def prev(off):
        return pl.BlockSpec((None, gc, 8, LANES), lambda b, i, j: (b, off + j, jnp.maximum(i * r8 - 1, 0), 0))

    def nxt(off):
        return pl.BlockSpec((None, gc, 8, LANES),
                            lambda b, i, j: (b, off + j, jnp.minimum((i + 1) * r8, S // 8 - 1), 0))

    def par(rows, off):
        return pl.BlockSpec((rows, gc * LANES), lambda b, i, j: (0, off + j))

    cb = conv_b.reshape(1, 2 * F)
    return pl.pallas_call(
        _convgate_kernel,
        grid=(B, S // ts, nj),
        in_specs=[cur(0), prev(0), nxt(0), cur(nj), prev(nj), nxt(nj), par(3, 0), par(3, nj), par(1, 0), par(1, nj)],
        out_specs=pl.BlockSpec((None, ts, gc * LANES), lambda b, i, j: (b, i, j)),
        out_shape=jax.ShapeDtypeStruct((B, S, F), jnp.bfloat16),
        compiler_params=_cparams(3),
        name="conv_gate",
    )(u, u, u, u, u, u, conv_w, conv_w, cb, cb)


def _hgrn_kernel(*refs, reverse, T):
    if reverse:
        (xq_ref, xf_ref, xi_ref, xg_ref, of_ref, llb_ref, l1m_ref, oml_ref, nw_ref, o_ref,
         st_ref, oin_ref, q_ref, k_ref, b_ref) = refs
    else:
        xq_ref, xf_ref, xi_ref, llb_ref, l1m_ref, oml_ref, o_ref, st_ref, oin_ref, q_ref, k_ref, b_ref = refs
    C = CHUNK
    nC = T // C
    half = C // 2
    quarter = C // 4

    @pl.when(pl.program_id(2) == 0)
    def _():
        st_ref[...] = jnp.zeros_like(st_ref)

    xf = xf_ref[...]
    llb, l1m, oml = llb_ref[...], l1m_ref[...], oml_ref[...]
    q = _silu(xq_ref[...])
    v16 = xi_ref[...].astype(jnp.bfloat16)
    e = jnp.exp(-jnp.abs(xf))
    y = l1m + jnp.minimum(xf, 0.0) - jnp.log(1.0 + e)
    logf = jnp.maximum(llb, y) + jnp.log(1.0 + jnp.exp(-jnp.abs(llb - y)))
    kk = oml * jnp.where(xf >= 0.0, e, 1.0) / (1.0 + e)
    span = jnp.sum((-logf).reshape(T // quarter, quarter, LANES), axis=1)
    exact_needed = jnp.max(span) > SAFE_SPAN

    rc = lax.broadcasted_iota(jnp.int32, (T, LANES), 0) & (C - 1)
    b = logf
    for sft in (1, 2, 4, 8, 16, 32):
        if reverse:
            b = b + jnp.where(rc < C - sft, pltpu.roll(b, T - sft, 0), 0.0)
        else:
            b = b + jnp.where(rc >= sft, pltpu.roll(b, sft, 0), 0.0)

    def rows(x, c):
        return x[c * C:(c + 1) * C, :]

    def b_last(bc):
        return bc[0:1, :] if reverse else bc[C - 1:C, :]

    ti = lax.broadcasted_iota(jnp.int32, (C, C), 0)
    si = lax.broadcasted_iota(jnp.int32, (C, C), 1)
    causal = (si >= ti) if reverse else (si <= ti)

    first = lax.broadcasted_iota(jnp.int32, (C, LANES), 0) < half
    q_side = first if reverse else jnp.logical_not(first)
    m0, m1, bnd = (quarter, half + quarter, half) if reverse else (quarter - 1, half + quarter - 1, half - 1)
    zero = jnp.zeros((C, LANES), jnp.bfloat16)

    def intra_factored(c, bc):
        mid = jnp.where(first, bc[m0:m0 + 1, :], bc[m1:m1 + 1, :])
        qa = (rows(q, c) * jnp.exp(bc - mid)).astype(jnp.bfloat16)
        ka = (rows(kk, c) * jnp.exp(mid - bc)).astype(jnp.bfloat16)
        eb = jnp.exp(-jnp.abs(bc - bc[bnd:bnd + 1, :]))
        qb = (rows(q, c) * eb).astype(jnp.bfloat16)
        kb = (rows(kk, c) * eb).astype(jnp.bfloat16)
        qcat = jnp.concatenate([jnp.where(first, qa, zero), jnp.where(first, zero, qa),
                                jnp.where(q_side, qb, zero)], axis=1)
        kcat = jnp.concatenate([jnp.where(first, ka, zero), jnp.where(first, zero, ka),
                                jnp.where(q_side, zero, kb)], axis=1)
        s = lax.dot_general(qcat, kcat, _NT, preferred_element_type=jnp.float32)
        s = jnp.where(causal, s, 0.0).astype(jnp.bfloat16)
        return jnp.dot(s, rows(v16, c), preferred_element_type=jnp.float32)

    def emit(rsl, o):
        if reverse:
            tot = of_ref[rsl, :] + o
            ms = jnp.mean(tot * tot, axis=-1, keepdims=True)
            yn = tot * lax.rsqrt(ms + EPS) * nw_ref[...]
            o_ref[rsl, :] = (yn * _silu(xg_ref[rsl, :])).astype(o_ref.dtype)
        else:
            o_ref[rsl, :] = o

    st = st_ref[...]
    for c in (range(nC - 1, -1, -1) if reverse else range(nC)):
        rsl = slice(c * C, (c + 1) * C)
        bc = rows(b, c)
        bl = b_last(bc)
        qa = (rows(q, c) * jnp.exp(bc)).astype(jnp.bfloat16)
        o_inter = lax.dot_general(qa, st.astype(jnp.bfloat16), _NT, preferred_element_type=jnp.float32)
        oin_ref[rsl, :] = o_inter
        emit(rsl, intra_factored(c, bc) + o_inter)
        kl = (rows(kk, c) * jnp.exp(bl - bc)).astype(jnp.bfloat16)
        st = st * jnp.exp(bl) + lax.dot_general(rows(v16, c), kl, _TN, preferred_element_type=jnp.float32)
    st_ref[...] = st

    @pl.when(exact_needed)
    def _():
        q_ref[...] = q
        k_ref[...] = kk
        b_ref[...] = b
        row = lax.broadcasted_iota(jnp.int32, (C, LANES), 0)

        def per_chunk(c, carry):
            rsl = pl.ds(pl.multiple_of(c * C, C), C)
            qc = q_ref[rsl, :]
            kc = k_ref[rsl, :]
            bc = b_ref[rsl, :]
            vc = xi_ref[rsl, :]

            def pair(d, acc):
                sh = ((C - d) % C) if reverse else d
                valid = (row < C - d) if reverse else (row >= d)
                dec = jnp.exp(jnp.where(valid, bc - pltpu.roll(bc, sh, 0), -jnp.inf))
                w = jnp.sum(qc * pltpu.roll(kc, sh, 0) * dec, axis=-1, keepdims=True)
                return acc + w * pltpu.roll(vc, sh, 0)

            oi = lax.fori_loop(0, C, pair, jnp.zeros((C, LANES), jnp.float32))
            emit(rsl, oi + oin_ref[rsl, :])
            return carry

        lax.fori_loop(0, nC, per_chunk, 0)


def hgrn_pass(zg, gate_params, *, reverse, o_fwd=None, norm_w=None, T=1024):
    B, _, S, _ = zg.shape
    T = min(T, S)
    nblk = S // T
    H = HA_HEADS

    def seq(c):
        return (nblk - 1 - c) if reverse else c

    def zspec(section):
        return pl.BlockSpec((None, None, T, LANES), lambda b, h, c: (b, section * H + h, seq(c), 0))

    hspec = pl.BlockSpec((None, None, T, LANES), lambda b, h, c: (b, h, seq(c), 0))
    pspec = pl.BlockSpec((1, LANES), lambda b, h, c: (0, h))
    llb, l1m, oml = gate_params
    if reverse:
        in_specs = [zspec(0), zspec(2), zspec(3), zspec(4), hspec,
                    pspec, pspec, pspec, pl.BlockSpec((1, LANES), lambda b, h, c: (0, 0))]
        args = (zg, zg, zg, zg, o_fwd, llb, l1m, oml, norm_w.reshape(1, LANES))
        out_spec = pl.BlockSpec((None, T, LANES), lambda b, h, c: (b, seq(c), h))
        out_shape = jax.ShapeDtypeStruct((B, S, HA_WIDTH), jnp.bfloat16)
    else:
        in_specs = [zspec(0), zspec(1), zspec(3), pspec, pspec, pspec]
        args = (zg, zg, zg, llb, l1m, oml)
        out_spec = hspec
        out_shape = jax.ShapeDtypeStruct((B, H, S, LANES), jnp.float32)
    blk = pltpu.VMEM((T, LANES), jnp.float32)
    return pl.pallas_call(
        functools.partial(_hgrn_kernel, reverse=reverse, T=T),
        grid=(B, H, nblk),
        in_specs=in_specs,
        out_specs=out_spec,
        out_shape=out_shape,
        scratch_shapes=[pltpu.VMEM((HA_HEAD, HA_HEAD), jnp.float32), blk, blk, blk, blk],
        compiler_params=_cparams(3),
        name="hgrn_bwd" if reverse else "hgrn_fwd",
    )(*args)


def _qknorm_kernel(q_ref, k_ref, v_ref, qw_ref, kw_ref, qo_ref, ko_ref, vo_ref, *, q_scale):
    qw = qw_ref[...] * q_scale
    kw = kw_ref[...]
    for g in range(q_ref.shape[0]):
        h, c = divmod(g, 2)
        sl = slice(c * DB_HEAD, (c + 1) * DB_HEAD)
        for x_ref, w, o_ref in ((q_ref, qw, qo_ref), (k_ref, kw, ko_ref)):
            x = x_ref[g]
            ms = jnp.mean(x * x, axis=-1, keepdims=True)
            o_ref[h, :, sl] = (x * lax.rsqrt(ms + EPS) * w).astype(o_ref.dtype)
        vo_ref[h, :, sl] = v_ref[g].astype(vo_ref.dtype)


def qk_norm(zg, q_norm_w, k_norm_w, *, ts=512):
    B, _, S, _ = zg.shape
    ts = min(ts, S)
    ng = DB_WIDTH // LANES
    first = 5 * HA_WIDTH // DB_WIDTH

    def zspec(sec):
        return pl.BlockSpec((None, ng, ts, LANES), lambda b, i: (b, first + sec, i, 0))

    wspec = pl.BlockSpec((1, DB_HEAD), lambda b, i: (0, 0))
    ospec = pl.BlockSpec((None, DB_HEADS, ts, 2 * DB_HEAD), lambda b, i: (b, 0, i, 0))
    oshape = jax.ShapeDtypeStruct((B, DB_HEADS, S, 2 * DB_HEAD), jnp.bfloat16)
    return pl.pallas_call(
        functools.partial(_qknorm_kernel, q_scale=DB_HEAD ** -0.5 * LOG2E),
        grid=(B, S // ts),
        in_specs=[zspec(0), zspec(1), zspec(2), wspec, wspec],
        out_specs=(ospec, ospec, ospec),
        out_shape=(oshape, oshape, oshape),
        compiler_params=_cparams(2),
        name="qk_norm",
    )(zg, zg, zg, q_norm_w.reshape(1, DB_HEAD), k_norm_w.reshape(1, DB_HEAD))


BIAS_TILES = 5


def _bias_kernel(rb_ref, o_ref, *, t):
    h = pl.program_id(0)
    offset = (pl.program_id(1) - BIAS_TILES // 2) * t
    qi = lax.broadcasted_iota(jnp.int32, (t, t), 0)
    kj = lax.broadcasted_iota(jnp.int32, (t, t), 1)
    rel = kj - qi + offset
    n = jnp.abs(rel)
    half = NUM_BUCKETS // 2

    def side(base):
        val = jnp.full((t, t), rb_ref[base + half - 1, h], jnp.float32)
        for c in range(half - 2, -1, -1):
            val = jnp.where(n < BUCKET_START[c + 1], rb_ref[base + c, h], val)
        return val

    o_ref[...] = jnp.where(rel > 0, side(half), side(0)) * LOG2E


def rel_bias_tiles(rel_bias, t):
    assert t >= 128
    return pl.pallas_call(
        functools.partial(_bias_kernel, t=t),
        grid=(DB_HEADS, BIAS_TILES),
        in_specs=[pl.BlockSpec(memory_space=pltpu.SMEM)],
        out_specs=pl.BlockSpec((None, None, t, t), lambda h, d: (h, d, 0, 0)),
        out_shape=jax.ShapeDtypeStruct((DB_HEADS, BIAS_TILES, t, t), jnp.float32),
        compiler_params=_cparams(2),
        name="rel_bias_tiles",
    )(rel_bias)


ATTN_ROWS = 32


def _attn_kernel(q_ref, k_ref, v_ref, bias_ref, lp_ref, sw_ref, o_ref,
                 m_ref, l_ref, acc_ref, m0_ref, l0_ref, acc0_ref, s_a, s_b, p_a, p_b, al_a, al_b, *, lambda_init):
    qb = pl.program_id(2)
    kb = pl.program_id(3)
    t = q_ref.shape[0]
    n_sub = k_ref.shape[0] // t
    rc = min(ATTN_ROWS, t)
    s_bufs, p_bufs, al_bufs = (s_a, s_b), (p_a, p_b), (al_a, al_b)

    @pl.when(kb == 0)
    def _():
        m_ref[...] = jnp.zeros_like(m_ref)
        l_ref[...] = jnp.zeros_like(l_ref)
        acc_ref[...] = jnp.zeros_like(acc_ref)

    m0_ref[...] = m_ref[...]
    l0_ref[...] = l_ref[...]
    acc0_ref[...] = acc_ref[...]

    def logits_into(j, c):
        s_bufs[j % 2][c] = lax.dot_general(q_ref[:, c * DB_HEAD:(c + 1) * DB_HEAD],
                                           k_ref[j * t:(j + 1) * t, c * DB_HEAD:(c + 1) * DB_HEAD],
                                           _NT, preferred_element_type=jnp.float32)

    def widen(x, width):
        return jnp.tile(x, (1, width // LANES))

    def sweep(safe):
        excess = jnp.zeros((rc, LANES), jnp.float32)
        for c in range(2):
            logits_into(0, c)
        for j in range(n_sub):
            s_buf, p_buf, al_buf = s_bufs[j % 2], p_bufs[j % 2], al_bufs[j % 2]
            tile_idx = jnp.clip(kb * n_sub + j - qb, -2, 2) + 2
            for c in range(2):
                if j + 1 < n_sub:
                    logits_into(j + 1, c)
                for i in range(t // rc):
                    rows = slice(i * rc, (i + 1) * rc)
                    crows = slice(c * t + i * rc, c * t + (i + 1) * rc)
                    s = s_buf[c, rows, :] + bias_ref[tile_idx, rows, :]
                    smax = jnp.max(s, axis=-1, keepdims=True)
                    r = m_ref[c, rows, :]
                    m_new = jnp.maximum(r, smax)
                    alpha = jnp.exp2(r - m_new)
                    if safe:
                        p = jnp.exp2(s - widen(m_new, t))
                        l_ref[c, rows, :] = alpha * l_ref[c, rows, :] + jnp.sum(p, axis=-1, keepdims=True)
                    else:
                        p = jnp.exp2(s - widen(r, t))
                        l_ref[c, rows, :] = alpha * (l_ref[c, rows, :] + jnp.sum(p, axis=-1, keepdims=True))
                        gap = smax - r
                        if j == 0:
                            gap = jnp.where(kb == 0, jnp.abs(gap), gap)
                        excess = jnp.maximum(excess, gap)
                    m_ref[c, rows, :] = m_new
                    p_buf[crows, :] = p.astype(jnp.bfloat16)
                    al_buf[crows, :] = alpha
                mrows = slice(c * t, (c + 1) * t)
                pv = jnp.dot(p_buf[mrows, :], v_ref[j * t:(j + 1) * t, :], preferred_element_type=jnp.float32)
                a2 = widen(al_buf[mrows, :], 2 * LANES)
                acc_ref[mrows, :] = (a2 * acc_ref[mrows, :] + pv) if safe else (a2 * (acc_ref[mrows, :] + pv))
        return excess

    redo = jnp.max(sweep(safe=False)) > SAFE_LOG2_GAP

    @pl.when(redo)
    def _():
        m_ref[...] = m0_ref[...] + jnp.where(kb == 0, -jnp.inf, 0.0)
        l_ref[...] = l0_ref[...]
        acc_ref[...] = acc0_ref[...]
        sweep(safe=True)

    @pl.when(kb == pl.num_programs(3) - 1)
    def _():
        lp = lp_ref[...]
        lam = (jnp.exp(jnp.sum(lp[0:1, :] * lp[1:2, :], axis=-1, keepdims=True))
               - jnp.exp(jnp.sum(lp[2:3, :] * lp[3:4, :], axis=-1, keepdims=True)) + lambda_init)
        o = (acc_ref[0:t, :] / jnp.tile(l_ref[0], (1, 2))
             - lam * (acc_ref[t:2 * t, :] / jnp.tile(l_ref[1], (1, 2))))
        ms = jnp.mean(o * o, axis=-1, keepdims=True)
        o_ref[...] = (o * lax.rsqrt(ms + EPS) * sw_ref[...] * (1.0 - lambda_init)).astype(o_ref.dtype)


def diff_attention(qn, kn, vb, bias_tiles, lam_params, subln_w, lambda_init):
    B, H, S, W = qn.shape
    t = bias_tiles.shape[-1]
    tk = min(ATTN_KEYS_PER_STEP, S)
    stat = pltpu.VMEM((2, t, LANES), jnp.float32)
    accs = pltpu.VMEM((2 * t, W), jnp.float32)
    return pl.pallas_call(
        functools.partial(_attn_kernel, lambda_init=lambda_init),
        grid=(B, H, S // t, S // tk),
        in_specs=[
            pl.BlockSpec((None, None, t, W), lambda b, h, i, j: (b, h, i, 0)),
            pl.BlockSpec((None, None, tk, W), lambda b, h, i, j: (b, h, j, 0)),
            pl.BlockSpec((None, None, tk, W), lambda b, h, i, j: (b, h, j, 0)),
            pl.BlockSpec((None, BIAS_TILES, t, t), lambda b, h, i, j: (h, 0, 0, 0)),
            pl.BlockSpec((4, DB_HEAD), lambda b, h, i, j: (0, 0)),
            pl.BlockSpec((1, W), lambda b, h, i, j: (0, 0)),
        ],
        out_specs=pl.BlockSpec((None, t, W), lambda b, h, i, j: (b, i, h)),
        out_shape=jax.ShapeDtypeStruct((B, S, DB_WIDTH), jnp.bfloat16),
        scratch_shapes=[stat, stat, accs, stat, stat, accs,
                        pltpu.VMEM((2, t, t), jnp.float32), pltpu.VMEM((2, t, t), jnp.float32),
                        pltpu.VMEM((2 * t, t), jnp.bfloat16), pltpu.VMEM((2 * t, t), jnp.bfloat16),
                        pltpu.VMEM((2 * t, LANES), jnp.float32), pltpu.VMEM((2 * t, LANES), jnp.float32)],
        compiler_params=_cparams(4),
        name="diff_attention",
    )(qn, kn, vb, bias_tiles, lam_params, subln_w.reshape(1, W))


ATTN_TILE = 512
ATTN_KEYS_PER_STEP = 2048


def _trunk(x, mod, layers, weights, bias_tiles):
    for l, p in enumerate(layers):
        m = mod[l]
        lambda_init = 0.8 - 0.6 * math.exp(-0.3 * l)
        h = norm_modulate(x, p["norm_attn_w"], m, 0, 1)
        zg = matmul_grouped(h, weights["w_in"], l, jnp.float32, tm=1024, tn=1024)
        o_fwd = hgrn_pass(zg, p["gates_fwd"], reverse=False)
        out_a = hgrn_pass(zg, p["gates_bwd"], reverse=True, o_fwd=o_fwd, norm_w=p["hgrn_norm_w"])
        qn, kn, vb = qk_norm(zg, p["q_norm_w"], p["k_norm_w"])
        out_b = diff_attention(qn, kn, vb, bias_tiles, p["diff_lambda"], p["diff_subln_w"], lambda_init)
        x = matmul_residual([out_a, out_b], weights["w_out"], l, x, m, 2, tm=1024, tn=1024)
        h = norm_modulate(x, p["norm_ffn_w"], m, 3, 4)
        ug = matmul_grouped(h, weights["w_up"], l, jnp.float32, tm=2048, tn=512)
        act = conv_gate(ug, p["conv_w"], p["conv_b"])
        x = matmul_residual([act], weights["w_down"], l, x, m, 5, tm=512, tn=512)
    return x


def kernel(x_prompt, x_sample, c_prompt, c_sample, w_ada, b_ada, norm_attn_w, w_in, hgrn_lb, hgrn_norm_w,
           q_norm_w, k_norm_w, diff_lambda, diff_subln_w, rel_bias, w_out, norm_ffn_w, w_up, conv_w, conv_b,
           w_down):
    L = w_ada.shape[0]
    D = x_prompt.shape[-1]
    Bp, Bs = c_prompt.shape[0], c_sample.shape[0]
    rows = -(-(Bp + Bs) // 8) * 8
    c_all = jnp.zeros((rows, D), jnp.float32).at[:Bp].set(c_prompt).at[Bp:Bp + Bs].set(c_sample)
    mod = adaln_mod(c_all, w_ada, b_ada)
    mod_p = mod[:, :Bp].reshape(L, Bp, 6, 1, D)
    mod_s = mod[:, Bp:Bp + Bs].reshape(L, Bs, 6, 1, D)

    lb_cum = jnp.cumsum(jax.nn.softmax(hgrn_lb.astype(jnp.float32), axis=0), axis=0)
    lb_all = lb_cum - lb_cum[0:1]
    log_lb, log_1m_lb, one_m_lb = jnp.log(lb_all), jnp.log1p(-lb_all), 1.0 - lb_all

    layers = []
    for l in range(L):
        def gates(d):
            return (log_lb[l, d][None], log_1m_lb[l, d][None], one_m_lb[l, d][None])
        layers.append(dict(
            norm_attn_w=norm_attn_w[l], gates_fwd=gates(0), gates_bwd=gates(1), hgrn_norm_w=hgrn_norm_w[l],
            q_norm_w=q_norm_w[l], k_norm_w=k_norm_w[l], diff_lambda=diff_lambda[l].astype(jnp.float32),
            diff_subln_w=diff_subln_w[l], norm_ffn_w=norm_ffn_w[l], conv_w=conv_w[l], conv_b=conv_b[l]))
    weights = dict(w_in=w_in.astype(jnp.bfloat16), w_out=w_out.astype(jnp.bfloat16),
                   w_up=w_up.astype(jnp.bfloat16), w_down=w_down.astype(jnp.bfloat16))
    rel_bias = rel_bias.astype(jnp.float32)
    tiles = {}
    outs = []
    for x, m in ((x_prompt, mod_p), (x_sample, mod_s)):
        S = x.shape[1]
        tile = min(ATTN_TILE, S)
        if tile not in tiles:
            tiles[tile] = rel_bias_tiles(rel_bias, tile)
        outs.append(_trunk(x, m, layers, weights, tiles[tile]))
    return tuple(outs)
```

```python
import functools
import math

import jax
import jax.numpy as jnp
from jax import lax
from jax.experimental import pallas as pl
from jax.experimental.pallas import tpu as pltpu

D_MODEL = 4096
HA_HEADS = 16
HA_HEAD = 128
HA_WIDTH = HA_HEADS * HA_HEAD
DB_HEADS = 8
DB_HEAD = 128
DB_WIDTH = DB_HEADS * 2 * DB_HEAD
IN_COLS = 5 * HA_WIDTH + 3 * DB_WIDTH
CHUNK = 64
NUM_BUCKETS = 32
EPS = 1e-6
BUCKET_START = (0, 1, 2, 3, 4, 5, 6, 7, 8, 12, 16, 23, 32, 46, 64, 91)

LANES = 128
V7X_VMEM_BYTES = 64 * 1024 * 1024
VMEM_LIMIT = V7X_VMEM_BYTES - 8 * 1024 * 1024

LOG2E = 1.0 / math.log(2.0)
SAFE_SPAN = 80.0
SAFE_LOG2_GAP = 100.0

_NT = (((1,), (1,)), ((), ()))
_TN = (((0,), (0,)), ((), ()))


def _cparams(n_axes):
    return pltpu.CompilerParams(
        dimension_semantics=("arbitrary",) * n_axes, vmem_limit_bytes=VMEM_LIMIT
    )


def _silu(x):
    return x * jax.nn.sigmoid(x)


def _ada_kernel(c_ref, w_ref, b_ref, o_ref):
    @pl.when(pl.program_id(2) == 0)
    def _():
        o_ref[...] = jnp.broadcast_to(b_ref[...], o_ref.shape)

    a = _silu(c_ref[...]).astype(jnp.bfloat16)
    o_ref[...] += jnp.dot(a, w_ref[...].astype(jnp.bfloat16), preferred_element_type=jnp.float32)


def adaln_mod(c_all, w_ada, b_ada, *, tk=1024, tn=2048):
    R, D = c_all.shape
    L, _, N = w_ada.shape
    return pl.pallas_call(
        _ada_kernel,
        grid=(L, N // tn, D // tk),
        in_specs=[
            pl.BlockSpec((R, tk), lambda l, j, k: (0, k)),
            pl.BlockSpec((None, tk, tn), lambda l, j, k: (l, k, j)),
            pl.BlockSpec((None, 1, tn), lambda l, j, k: (l, 0, j)),
        ],
        out_specs=pl.BlockSpec((None, R, tn), lambda l, j, k: (l, 0, j)),
        out_shape=jax.ShapeDtypeStruct((L, R, N), jnp.float32),
        compiler_params=_cparams(3),
        name="adaln_mod",
    )(c_all, w_ada, b_ada.reshape(L, 1, N))


def _normmod_kernel(x_ref, w_ref, sc_ref, sh_ref, o_ref):
    x = x_ref[...]
    ms = jnp.mean(x * x, axis=-1, keepdims=True)
    y = x * lax.rsqrt(ms + EPS) * w_ref[...]
    o_ref[...] = (y * (1.0 + sc_ref[...]) + sh_ref[...]).astype(o_ref.dtype)


def norm_modulate(x, w, mod, shift_idx, scale_idx, *, ts=512):
    B, S, D = x.shape
    ts = min(ts, S)
    return pl.pallas_call(
        _normmod_kernel,
        grid=(B, S // ts),
        in_specs=[
            pl.BlockSpec((None, ts, D), lambda b, i: (b, i, 0)),
            pl.BlockSpec((1, D), lambda b, i: (0, 0)),
            pl.BlockSpec((None, None, 1, D), lambda b, i: (b, scale_idx, 0, 0)),
            pl.BlockSpec((None, None, 1, D), lambda b, i: (b, shift_idx, 0, 0)),
        ],
        out_specs=pl.BlockSpec((None, ts, D), lambda b, i: (b, i, 0)),
        out_shape=jax.ShapeDtypeStruct((B, S, D), jnp.bfloat16),
        compiler_params=_cparams(2),
        name="norm_modulate",
    )(x, w.reshape(1, D), mod, mod)


def _mm_grouped_kernel(a_ref, w_ref, o_ref):
    acc = jnp.dot(a_ref[...], w_ref[...], preferred_element_type=jnp.float32)
    for g in range(o_ref.shape[0]):
        o_ref[g] = acc[:, g * LANES:(g + 1) * LANES].astype(o_ref.dtype)


def matmul_grouped(a, w, layer, out_dtype, *, tm, tn):
    B, S, K = a.shape
    N = w.shape[2]
    tm, tn = min(tm, S), min(tn, N)
    return pl.pallas_call(
        _mm_grouped_kernel,
        grid=(B, S // tm, N // tn),
        in_specs=[
            pl.BlockSpec((None, tm, K), lambda b, i, j: (b, i, 0)),
            pl.BlockSpec((None, K, tn), lambda b, i, j: (layer, 0, j)),
        ],
        out_specs=pl.BlockSpec((None, tn // LANES, tm, LANES), lambda b, i, j: (b, j, i, 0)),
        out_shape=jax.ShapeDtypeStruct((B, N // LANES, S, LANES), out_dtype),
        compiler_params=_cparams(3),
        name="matmul_grouped",
    )(a, w)


def _mm_res_kernel(*refs, n_lhs):
    a_refs = refs[:n_lhs]
    w_ref, x_ref, g_ref, o_ref = refs[n_lhs:]
    acc = None
    k0 = 0
    for a_ref in a_refs:
        kw = a_ref.shape[-1]
        part = jnp.dot(a_ref[...], w_ref[k0:k0 + kw, :], preferred_element_type=jnp.float32)
        acc = part if acc is None else acc + part
        k0 += kw
    o_ref[...] = x_ref[...] + g_ref[...] * acc


def matmul_residual(lhs_parts, w, layer, x, mod, gate_idx, *, tm, tn):
    B, S, N = x.shape
    K = w.shape[1]
    tm, tn = min(tm, S), min(tn, N)
    n_lhs = len(lhs_parts)
    in_specs = [pl.BlockSpec((None, tm, a.shape[-1]), lambda b, i, j: (b, i, 0)) for a in lhs_parts]
    in_specs += [
        pl.BlockSpec((None, K, tn), lambda b, i, j: (layer, 0, j)),
        pl.BlockSpec((None, tm, tn), lambda b, i, j: (b, i, j)),
        pl.BlockSpec((None, None, 1, tn), lambda b, i, j: (b, gate_idx, 0, j)),
    ]
    return pl.pallas_call(
        functools.partial(_mm_res_kernel, n_lhs=n_lhs),
        grid=(B, S // tm, N // tn),
        in_specs=in_specs,
        out_specs=pl.BlockSpec((None, tm, tn), lambda b, i, j: (b, i, j)),
        out_shape=jax.ShapeDtypeStruct((B, S, N), jnp.float32),
        compiler_params=_cparams(3),
        name="matmul_residual",
    )(*lhs_parts, w, x, mod)


CONV_ROWS = 64


def _convgate_kernel(uv_ref, uvp_ref, uvn_ref, ug_ref, ugp_ref, ugn_ref, wv_ref, wg_ref, bv_ref, bg_ref, o_ref):
    i = pl.program_id(1)
    last = pl.num_programs(1) - 1
    gc, ts, _ = uv_ref.shape
    rc = min(CONV_ROWS, ts)
    n_chunk = ts // rc
    row = lax.broadcasted_iota(jnp.int32, (rc, LANES), 0)

    def conv(g, c, cur_ref, prev_ref, next_ref, w_ref, b_ref):
        lanes = slice(g * LANES, (g + 1) * LANES)
        r0 = c * rc
        cur = cur_ref[g, r0:r0 + rc, :]
        if c == 0:
            prow = jnp.where(i > 0, prev_ref[g][7:8, :], 0.0)
            up = jnp.where(row == 0, prow, pltpu.roll(cur, 1, 0))
        else:
            up = cur_ref[g, r0 - 1:r0 - 1 + rc, :]
        if c == n_chunk - 1:
            nrow = jnp.where(i < last, next_ref[g][0:1, :], 0.0)
            dn = jnp.where(row == rc - 1, nrow, pltpu.roll(cur, rc - 1, 0))
        else:
            dn = cur_ref[g, r0 + 1:r0 + 1 + rc, :]
        return up * w_ref[0:1, lanes] + cur * w_ref[1:2, lanes] + dn * w_ref[2:3, lanes] + b_ref[:, lanes]

    for g in range(gc):
        for c in range(n_chunk):
            val = conv(g, c, uv_ref, uvp_ref, uvn_ref, wv_ref, bv_ref)
            gate = conv(g, c, ug_ref, ugp_ref, ugn_ref, wg_ref, bg_ref)
            o_ref[c * rc:(c + 1) * rc, g * LANES:(g + 1) * LANES] = (_silu(gate) * val).astype(o_ref.dtype)


def conv_gate(u, conv_w, conv_b, *, ts=2048, gc=2):
    B, G2, S, _ = u.shape
    nf = G2 // 2
    F = nf * LANES
    ts = min(ts, S)
    nj = nf // gc
    r8 = ts // 8

    def cur(off):
        return pl.BlockSpec((None, gc, ts, LANES), lambda b, i, j: (b, off + j, i, 0))

    def prev(off):
        return pl.BlockSpec((None, gc, 8, LANES), lambda b, i, j: (b, off + j, jnp.maximum(i * r8 - 1, 0), 0))

    def nxt(off):
        return pl.BlockSpec((None, gc, 8, LANES),
                            lambda b, i, j: (b, off + j, jnp.minimum((i + 1) * r8, S // 8 - 1), 0))

    def par(rows, off):
        return pl.BlockSpec((rows, gc * LANES), lambda b, i, j: (0, off + j))

    cb = conv_b.reshape(1, 2 * F)
    return pl.pallas_call(
        _convgate_kernel,
        grid=(B, S // ts, nj),
        in_specs=[cur(0), prev(0), nxt(0), cur(nj), prev(nj), nxt(nj), par(3, 0), par(3, nj), par(1, 0), par(1, nj)],
        out_specs=pl.BlockSpec((None, ts, gc * LANES), lambda b, i, j: (b, i, j)),
        out_shape=jax.ShapeDtypeStruct((B, S, F), jnp.bfloat16),
        compiler_params=_cparams(3),
        name="conv_gate",
    )(u, u, u, u, u, u, conv_w, conv_w, cb, cb)


def _hgrn_kernel(*refs, reverse, T):
    if reverse:
        (xq_ref, xf_ref, xi_ref, xg_ref, of_ref, llb_ref, l1m_ref, oml_ref, nw_ref, o_ref,
         st_ref, oin_ref, q_ref, k_ref, b_ref) = refs
    else:
        xq_ref, xf_ref, xi_ref, llb_ref, l1m_ref, oml_ref, o_ref, st_ref, oin_ref, q_ref, k_ref, b_ref = refs
    C = CHUNK
    nC = T // C
    half = C // 2
    quarter = C // 4

    @pl.when(pl.program_id(2) == 0)
    def _():
        st_ref[...] = jnp.zeros_like(st_ref)

    xf = xf_ref[...]
    llb, l1m, oml = llb_ref[...], l1m_ref[...], oml_ref[...]
    q = _silu(xq_ref[...])
    v16 = xi_ref[...].astype(jnp.bfloat16)
    e = jnp.exp(-jnp.abs(xf))
    y = l1m + jnp.minimum(xf, 0.0) - jnp.log(1.0 + e)
    logf = jnp.maximum(llb, y) + jnp.log(1.0 + jnp.exp(-jnp.abs(llb - y)))
    kk = oml * jnp.where(xf >= 0.0, e, 1.0) / (1.0 + e)
    span = jnp.sum((-logf).reshape(T // quarter, quarter, LANES), axis=1)
    exact_needed = jnp.max(span) > SAFE_SPAN

    rc = lax.broadcasted_iota(jnp.int32, (T, LANES), 0) & (C - 1)
    b = logf
    for sft in (1, 2, 4, 8, 16, 32):
        if reverse:
            b = b + jnp.where(rc < C - sft, pltpu.roll(b, T - sft, 0), 0.0)
        else:
            b = b + jnp.where(rc >= sft, pltpu.roll(b, sft, 0), 0.0)

    def rows(x, c):
        return x[c * C:(c + 1) * C, :]

    def b_last(bc):
        return bc[0:1, :] if reverse else bc[C - 1:C, :]

    ti = lax.broadcasted_iota(jnp.int32, (C, C), 0)
    si = lax.broadcasted_iota(jnp.int32, (C, C), 1)
    causal = (si >= ti) if reverse else (si <= ti)

    first = lax.broadcasted_iota(jnp.int32, (C, LANES), 0) < half
    q_side = first if reverse else jnp.logical_not(first)
    m0, m1, bnd = (quarter, half + quarter, half) if reverse else (quarter - 1, half + quarter - 1, half - 1)
    zero = jnp.zeros((C, LANES), jnp.bfloat16)

    def intra_factored(c, bc):
        mid = jnp.where(first, bc[m0:m0 + 1, :], bc[m1:m1 + 1, :])
        qa = (rows(q, c) * jnp.exp(bc - mid)).astype(jnp.bfloat16)
        ka = (rows(kk, c) * jnp.exp(mid - bc)).astype(jnp.bfloat16)
        eb = jnp.exp(-jnp.abs(bc - bc[bnd:bnd + 1, :]))
        qb = (rows(q, c) * eb).astype(jnp.bfloat16)
        kb = (rows(kk, c) * eb).astype(jnp.bfloat16)
        qcat = jnp.concatenate([jnp.where(first, qa, zero), jnp.where(first, zero, qa),
                                jnp.where(q_side, qb, zero)], axis=1)
        kcat = jnp.concatenate([jnp.where(first, ka, zero), jnp.where(first, zero, ka),
                                jnp.where(q_side, zero, kb)], axis=1)
        s = lax.dot_general(qcat, kcat, _NT, preferred_element_type=jnp.float32)
        s = jnp.where(causal, s, 0.0).astype(jnp.bfloat16)
        return jnp.dot(s, rows(v16, c), preferred_element_type=jnp.float32)

    def emit(rsl, o):
        if reverse:
            tot = of_ref[rsl, :] + o
            ms = jnp.mean(tot * tot, axis=-1, keepdims=True)
            yn = tot * lax.rsqrt(ms + EPS) * nw_ref[...]
            o_ref[rsl, :] = (yn * _silu(xg_ref[rsl, :])).astype(o_ref.dtype)
        else:
            o_ref[rsl, :] = o

    st = st_ref[...]
    for c in (range(nC - 1, -1, -1) if reverse else range(nC)):
        rsl = slice(c * C, (c + 1) * C)
        bc = rows(b, c)
        bl = b_last(bc)
        qa = (rows(q, c) * jnp.exp(bc)).astype(jnp.bfloat16)
        o_inter = lax.dot_general(qa, st.astype(jnp.bfloat16), _NT, preferred_element_type=jnp.float32)
        oin_ref[rsl, :] = o_inter
        emit(rsl, intra_factored(c, bc) + o_inter)
        kl = (rows(kk, c) * jnp.exp(bl - bc)).astype(jnp.bfloat16)
        st = st * jnp.exp(bl) + lax.dot_general(rows(v16, c), kl, _TN, preferred_element_type=jnp.float32)
    st_ref[...] = st

    @pl.when(exact_needed)
    def _():
        q_ref[...] = q
        k_ref[...] = kk
        b_ref[...] = b
        row = lax.broadcasted_iota(jnp.int32, (C, LANES), 0)

        def per_chunk(c, carry):
            rsl = pl.ds(pl.multiple_of(c * C, C), C)
            qc = q_ref[rsl, :]
            kc = k_ref[rsl, :]
            bc = b_ref[rsl, :]
            vc = xi_ref[rsl, :]

            def pair(d, acc):
                sh = ((C - d) % C) if reverse else d
                valid = (row < C - d) if reverse else (row >= d)
                dec = jnp.exp(jnp.where(valid, bc - pltpu.roll(bc, sh, 0), -jnp.inf))
                w = jnp.sum(qc * pltpu.roll(kc, sh, 0) * dec, axis=-1, keepdims=True)
                return acc + w * pltpu.roll(vc, sh, 0)

            oi = lax.fori_loop(0, C, pair, jnp.zeros((C, LANES), jnp.float32))
            emit(rsl, oi + oin_ref[rsl, :])
            return carry

        lax.fori_loop(0, nC, per_chunk, 0)


def hgrn_pass(zg, gate_params, *, reverse, o_fwd=None, norm_w=None, T=2048):
    B, _, S, _ = zg.shape
    T = min(T, S)
    nblk = S // T
    H = HA_HEADS

    def seq(c):
        return (nblk - 1 - c) if reverse else c

    def zspec(section):
        return pl.BlockSpec((None, None, T, LANES), lambda b, h, c: (b, section * H + h, seq(c), 0))

    hspec = pl.BlockSpec((None, None, T, LANES), lambda b, h, c: (b, h, seq(c), 0))
    pspec = pl.BlockSpec((1, LANES), lambda b, h, c: (0, h))
    llb, l1m, oml = gate_params
    if reverse:
        in_specs = [zspec(0), zspec(2), zspec(3), zspec(4), hspec,
                    pspec, pspec, pspec, pl.BlockSpec((1, LANES), lambda b, h, c: (0, 0))]
        args = (zg, zg, zg, zg, o_fwd, llb, l1m, oml, norm_w.reshape(1, LANES))
        out_spec = pl.BlockSpec((None, T, LANES), lambda b, h, c: (b, seq(c), h))
        out_shape = jax.ShapeDtypeStruct((B, S, HA_WIDTH), jnp.bfloat16)
    else:
        in_specs = [zspec(0), zspec(1), zspec(3), pspec, pspec, pspec]
        args = (zg, zg, zg, llb, l1m, oml)
        out_spec = hspec
        out_shape = jax.ShapeDtypeStruct((B, H, S, LANES), jnp.float32)
    blk = pltpu.VMEM((T, LANES), jnp.float32)
    return pl.pallas_call(
        functools.partial(_hgrn_kernel, reverse=reverse, T=T),
        grid=(B, H, nblk),
        in_specs=in_specs,
        out_specs=out_spec,
        out_shape=out_shape,
        scratch_shapes=[pltpu.VMEM((HA_HEAD, HA_HEAD), jnp.float32), blk, blk, blk, blk],
        compiler_params=_cparams(3),
        name="hgrn_bwd" if reverse else "hgrn_fwd",
    )(*args)


def _qknorm_kernel(q_ref, k_ref, v_ref, qw_ref, kw_ref, qo_ref, ko_ref, vo_ref, *, q_scale):
    qw = qw_ref[...] * q_scale
    kw = kw_ref[...]
    for g in range(q_ref.shape[0]):
        h, c = divmod(g, 2)
        sl = slice(c * DB_HEAD, (c + 1) * DB_HEAD)
        for x_ref, w, o_ref in ((q_ref, qw, qo_ref), (k_ref, kw, ko_ref)):
            x = x_ref[g]
            ms = jnp.mean(x * x, axis=-1, keepdims=True)
            o_ref[h, :, sl] = (x * lax.rsqrt(ms + EPS) * w).astype(o_ref.dtype)
        vo_ref[h, :, sl] = v_ref[g].astype(vo_ref.dtype)


def qk_norm(zg, q_norm_w, k_norm_w, *, ts=512):
    B, _, S, _ = zg.shape
    ts = min(ts, S)
    ng = DB_WIDTH // LANES
    first = 5 * HA_WIDTH // DB_WIDTH

    def zspec(sec):
        return pl.BlockSpec((None, ng, ts, LANES), lambda b, i: (b, first + sec, i, 0))

    wspec = pl.BlockSpec((1, DB_HEAD), lambda b, i: (0, 0))
    ospec = pl.BlockSpec((None, DB_HEADS, ts, 2 * DB_HEAD), lambda b, i: (b, 0, i, 0))
    oshape = jax.ShapeDtypeStruct((B, DB_HEADS, S, 2 * DB_HEAD), jnp.bfloat16)
    return pl.pallas_call(
        functools.partial(_qknorm_kernel, q_scale=DB_HEAD ** -0.5 * LOG2E),
        grid=(B, S // ts),
        in_specs=[zspec(0), zspec(1), zspec(2), wspec, wspec],
        out_specs=(ospec, ospec, ospec),
        out_shape=(oshape, oshape, oshape),
        compiler_params=_cparams(2),
        name="qk_norm",
    )(zg, zg, zg, q_norm_w.reshape(1, DB_HEAD), k_norm_w.reshape(1, DB_HEAD))


BIAS_TILES = 5


def _bias_kernel(rb_ref, o_ref, *, t):
    h = pl.program_id(0)
    offset = (pl.program_id(1) - BIAS_TILES // 2) * t
    qi = lax.broadcasted_iota(jnp.int32, (t, t), 0)
    kj = lax.broadcasted_iota(jnp.int32, (t, t), 1)
    rel = kj - qi + offset
    n = jnp.abs(rel)
    half = NUM_BUCKETS // 2

    def side(base):
        val = jnp.full((t, t), rb_ref[base + half - 1, h], jnp.float32)
        for c in range(half - 2, -1, -1):
            val = jnp.where(n < BUCKET_START[c + 1], rb_ref[base + c, h], val)
        return val

    o_ref[...] = jnp.where(rel > 0, side(half), side(0)) * LOG2E


def rel_bias_tiles(rel_bias, t):
    assert t >= 128
    return pl.pallas_call(
        functools.partial(_bias_kernel, t=t),
        grid=(DB_HEADS, BIAS_TILES),
        in_specs=[pl.BlockSpec(memory_space=pltpu.SMEM)],
        out_specs=pl.BlockSpec((None, None, t, t), lambda h, d: (h, d, 0, 0)),
        out_shape=jax.ShapeDtypeStruct((DB_HEADS, BIAS_TILES, t, t), jnp.float32),
        compiler_params=_cparams(2),
        name="rel_bias_tiles",
    )(rel_bias)


ATTN_ROWS = 32


def _attn_kernel(q_ref, k_ref, v_ref, bias_ref, lp_ref, sw_ref, o_ref,
                 m_ref, l_ref, acc_ref, m0_ref, l0_ref, acc0_ref, s_a, s_b, p_a, p_b, al_a, al_b, *, lambda_init):
    qb = pl.program_id(2)
    kb = pl.program_id(3)
    t = q_ref.shape[0]
    n_sub = k_ref.shape[0] // t
    rc = min(ATTN_ROWS, t)
    s_bufs, p_bufs, al_bufs = (s_a, s_b), (p_a, p_b), (al_a, al_b)

    @pl.when(kb == 0)
    def _():
        m_ref[...] = jnp.zeros_like(m_ref)
        l_ref[...] = jnp.zeros_like(l_ref)
        acc_ref[...] = jnp.zeros_like(acc_ref)

    m0_ref[...] = m_ref[...]
    l0_ref[...] = l_ref[...]
    acc0_ref[...] = acc_ref[...]

    def logits_into(j, c):
        s_bufs[j % 2][c] = lax.dot_general(q_ref[:, c * DB_HEAD:(c + 1) * DB_HEAD],
                                           k_ref[j * t:(j + 1) * t, c * DB_HEAD:(c + 1) * DB_HEAD],
                                           _NT, preferred_element_type=jnp.float32)

    def widen(x, width):
        return jnp.tile(x, (1, width // LANES))

    def sweep(safe):
        excess = jnp.zeros((rc, LANES), jnp.float32)
        for c in range(2):
            logits_into(0, c)
        for j in range(n_sub):
            s_buf, p_buf, al_buf = s_bufs[j % 2], p_bufs[j % 2], al_bufs[j % 2]
            tile_idx = jnp.clip(kb * n_sub + j - qb, -2, 2) + 2
            for c in range(2):
                if j + 1 < n_sub:
                    logits_into(j + 1, c)
                for i in range(t // rc):
                    rows = slice(i * rc, (i + 1) * rc)
                    crows = slice(c * t + i * rc, c * t + (i + 1) * rc)
                    s = s_buf[c, rows, :] + bias_ref[tile_idx, rows, :]
                    smax = jnp.max(s, axis=-1, keepdims=True)
                    r = m_ref[c, rows, :]
                    m_new = jnp.maximum(r, smax)
                    alpha = jnp.exp2(r - m_new)
                    if safe:
                        p = jnp.exp2(s - widen(m_new, t))
                        l_ref[c, rows, :] = alpha * l_ref[c, rows, :] + jnp.sum(p, axis=-1, keepdims=True)
                    else:
                        p = jnp.exp2(s - widen(r, t))
                        l_ref[c, rows, :] = alpha * (l_ref[c, rows, :] + jnp.sum(p, axis=-1, keepdims=True))
                        gap = smax - r
                        if j == 0:
                            gap = jnp.where(kb == 0, jnp.abs(gap), gap)
                        excess = jnp.maximum(excess, gap)
                    m_ref[c, rows, :] = m_new
                    p_buf[crows, :] = p.astype(jnp.bfloat16)
                    al_buf[crows, :] = alpha
                mrows = slice(c * t, (c + 1) * t)
                pv = jnp.dot(p_buf[mrows, :], v_ref[j * t:(j + 1) * t, :], preferred_element_type=jnp.float32)
                a2 = widen(al_buf[mrows, :], 2 * LANES)
                acc_ref[mrows, :] = (a2 * acc_ref[mrows, :] + pv) if safe else (a2 * (acc_ref[mrows, :] + pv))
        return excess

    redo = jnp.max(sweep(safe=False)) > SAFE_LOG2_GAP

    @pl.when(redo)
    def _():
        m_ref[...] = m0_ref[...] + jnp.where(kb == 0, -jnp.inf, 0.0)
        l_ref[...] = l0_ref[...]
        acc_ref[...] = acc0_ref[...]
        sweep(safe=True)

    @pl.when(kb == pl.num_programs(3) - 1)
    def _():
        lp = lp_ref[...]
        lam = (jnp.exp(jnp.sum(lp[0:1, :] * lp[1:2, :], axis=-1, keepdims=True))
               - jnp.exp(jnp.sum(lp[2:3, :] * lp[3:4, :], axis=-1, keepdims=True)) + lambda_init)
        o = (acc_ref[0:t, :] / jnp.tile(l_ref[0], (1, 2))
             - lam * (acc_ref[t:2 * t, :] / jnp.tile(l_ref[1], (1, 2))))
        ms = jnp.mean(o * o, axis=-1, keepdims=True)
        o_ref[...] = (o * lax.rsqrt(ms + EPS) * sw_ref[...] * (1.0 - lambda_init)).astype(o_ref.dtype)


def diff_attention(qn, kn, vb, bias_tiles, lam_params, subln_w, lambda_init):
    B, H, S, W = qn.shape
    t = bias_tiles.shape[-1]
    tk = min(ATTN_KEYS_PER_STEP, S)
    stat = pltpu.VMEM((2, t, LANES), jnp.float32)
    accs = pltpu.VMEM((2 * t, W), jnp.float32)
    return pl.pallas_call(
        functools.partial(_attn_kernel, lambda_init=lambda_init),
        grid=(B, H, S // t, S // tk),
        in_specs=[
            pl.BlockSpec((None, None, t, W), lambda b, h, i, j: (b, h, i, 0)),
            pl.BlockSpec((None, None, tk, W), lambda b, h, i, j: (b, h, j, 0)),
            pl.BlockSpec((None, None, tk, W), lambda b, h, i, j: (b, h, j, 0)),
            pl.BlockSpec((None, BIAS_TILES, t, t), lambda b, h, i, j: (h, 0, 0, 0)),
            pl.BlockSpec((4, DB_HEAD), lambda b, h, i, j: (0, 0)),
            pl.BlockSpec((1, W), lambda b, h, i, j: (0, 0)),
        ],
        out_specs=pl.BlockSpec((None, t, W), lambda b, h, i, j: (b, i, h)),
        out_shape=jax.ShapeDtypeStruct((B, S, DB_WIDTH), jnp.bfloat16),
        scratch_shapes=[stat, stat, accs, stat, stat, accs,
                        pltpu.VMEM((2, t, t), jnp.float32), pltpu.VMEM((2, t, t), jnp.float32),
                        pltpu.VMEM((2 * t, t), jnp.bfloat16), pltpu.VMEM((2 * t, t), jnp.bfloat16),
                        pltpu.VMEM((2 * t, LANES), jnp.float32), pltpu.VMEM((2 * t, LANES), jnp.float32)],
        compiler_params=_cparams(4),
        name="diff_attention",
    )(qn, kn, vb, bias_tiles, lam_params, subln_w.reshape(1, W))


ATTN_TILE = 512
ATTN_KEYS_PER_STEP = 4096


def _trunk(x, mod, layers, weights, bias_tiles):
    for l, p in enumerate(layers):
        m = mod[l]
        lambda_init = 0.8 - 0.6 * math.exp(-0.3 * l)
        h = norm_modulate(x, p["norm_attn_w"], m, 0, 1)
        zg = matmul_grouped(h, weights["w_in"], l, jnp.float32, tm=1024, tn=1024)
        o_fwd = hgrn_pass(zg, p["gates_fwd"], reverse=False)
        out_a = hgrn_pass(zg, p["gates_bwd"], reverse=True, o_fwd=o_fwd, norm_w=p["hgrn_norm_w"])
        qn, kn, vb = qk_norm(zg, p["q_norm_w"], p["k_norm_w"])
        out_b = diff_attention(qn, kn, vb, bias_tiles, p["diff_lambda"], p["diff_subln_w"], lambda_init)
        x = matmul_residual([out_a, out_b], weights["w_out"], l, x, m, 2, tm=1024, tn=1024)
        h = norm_modulate(x, p["norm_ffn_w"], m, 3, 4)
        ug = matmul_grouped(h, weights["w_up"], l, jnp.float32, tm=2048, tn=512)
        act = conv_gate(ug, p["conv_w"], p["conv_b"])
        x = matmul_residual([act], weights["w_down"], l, x, m, 5, tm=512, tn=512)
    return x


def kernel(x_prompt, x_sample, c_prompt, c_sample, w_ada, b_ada, norm_attn_w, w_in, hgrn_lb, hgrn_norm_w,
           q_norm_w, k_norm_w, diff_lambda, diff_subln_w, rel_bias, w_out, norm_ffn_w, w_up, conv_w, conv_b,
           w_down):
    L = w_ada.shape[0]
    D = x_prompt.shape[-1]
    Bp, Bs = c_prompt.shape[0], c_sample.shape[0]
    rows = -(-(Bp + Bs) // 8) * 8
    c_all = jnp.zeros((rows, D), jnp.float32).at[:Bp].set(c_prompt).at[Bp:Bp + Bs].set(c_sample)
    mod = adaln_mod(c_all, w_ada, b_ada)
    mod_p = mod[:, :Bp].reshape(L, Bp, 6, 1, D)
    mod_s = mod[:, Bp:Bp + Bs].reshape(L, Bs, 6, 1, D)

    lb_cum = jnp.cumsum(jax.nn.softmax(hgrn_lb.astype(jnp.float32), axis=0), axis=0)
    lb_all = lb_cum - lb_cum[0:1]
    log_lb, log_1m_lb, one_m_lb = jnp.log(lb_all), jnp.log1p(-lb_all), 1.0 - lb_all

    layers = []
    for l in range(L):
        def gates(d):
            return (log_lb[l, d][None], log_1m_lb[l, d][None], one_m_lb[l, d][None])
        layers.append(dict(
            norm_attn_w=norm_attn_w[l], gates_fwd=gates(0), gates_bwd=gates(1), hgrn_norm_w=hgrn_norm_w[l],
            q_norm_w=q_norm_w[l], k_norm_w=k_norm_w[l], diff_lambda=diff_lambda[l].astype(jnp.float32),
            diff_subln_w=diff_subln_w[l], norm_ffn_w=norm_ffn_w[l], conv_w=conv_w[l], conv_b=conv_b[l]))
    weights = dict(w_in=w_in.astype(jnp.bfloat16), w_out=w_out.astype(jnp.bfloat16),
                   w_up=w_up.astype(jnp.bfloat16), w_down=w_down.astype(jnp.bfloat16))
    rel_bias = rel_bias.astype(jnp.float32)
    tiles = {}
    outs = []
    for x, m in ((x_prompt, mod_p), (x_sample, mod_s)):
        S = x.shape[1]
        tile = min(ATTN_TILE, S)
        if tile not in tiles:
            tiles[tile] = rel_bias_tiles(rel_bias, tile)
        outs.append(_trunk(x, m, layers, weights, tiles[tile]))
    return tuple(outs)
```

```python
import functools
import math

import jax
import jax.numpy as jnp
from jax import lax
from jax.experimental import pallas as pl
from jax.experimental.pallas import tpu as pltpu

D_MODEL = 4096
HA_HEADS = 16
HA_HEAD = 128
HA_WIDTH = HA_HEADS * HA_HEAD
DB_HEADS = 8
DB_HEAD = 128
DB_WIDTH = DB_HEADS * 2 * DB_HEAD
IN_COLS = 5 * HA_WIDTH + 3 * DB_WIDTH
CHUNK = 64
NUM_BUCKETS = 32
EPS = 1e-6
BUCKET_START = (0, 1, 2, 3, 4, 5, 6, 7, 8, 12, 16, 23, 32, 46, 64, 91)

LANES = 128
SUBLANES = 8
V7X_VMEM_BYTES = 64 * 1024 * 1024
VMEM_LIMIT = V7X_VMEM_BYTES - 8 * 1024 * 1024

W_IN_TILE = (1024, 1024)
W_OUT_TILE = (1024, 1024)
W_UP_TILE = (2048, 512)
W_DOWN_TILE = (512, 512)
HGRN_BLOCK = 2048
CONV_TILE_ROWS = 2048
CONV_ROWS = 64
ATTN_TILE = 512
ATTN_KEYS_PER_STEP = 4096
ATTN_ROWS = 32

LOG2E = 1.0 / math.log(2.0)
SAFE_SPAN = 80.0
SAFE_LOG2_GAP = 100.0

_NT = (((1,), (1,)), ((), ()))
_TN = (((0,), (0,)), ((), ()))


def _cparams(n_axes):
    return pltpu.CompilerParams(
        dimension_semantics=("arbitrary",) * n_axes, vmem_limit_bytes=VMEM_LIMIT
    )


def _silu(x):
    return x * jax.nn.sigmoid(x)


def _ada_kernel(c_ref, w_ref, b_ref, o_ref):
    @pl.when(pl.program_id(2) == 0)
    def _():
        o_ref[...] = jnp.broadcast_to(b_ref[...], o_ref.shape)

    a = _silu(c_ref[...]).astype(jnp.bfloat16)
    o_ref[...] += jnp.dot(a, w_ref[...].astype(jnp.bfloat16), preferred_element_type=jnp.float32)


def adaln_mod(c_all, w_ada, b_ada, *, tk=1024, tn=2048):
    R, D = c_all.shape
    L, _, N = w_ada.shape
    return pl.pallas_call(
        _ada_kernel,
        grid=(L, N // tn, D // tk),
        in_specs=[
            pl.BlockSpec((R, tk), lambda l, j, k: (0, k)),
            pl.BlockSpec((None, tk, tn), lambda l, j, k: (l, k, j)),
            pl.BlockSpec((None, 1, tn), lambda l, j, k: (l, 0, j)),
        ],
        out_specs=pl.BlockSpec((None, R, tn), lambda l, j, k: (l, 0, j)),
        out_shape=jax.ShapeDtypeStruct((L, R, N), jnp.float32),
        compiler_params=_cparams(3),
        name="adaln_mod",
    )(c_all, w_ada, b_ada.reshape(L, 1, N))


def _normmod_kernel(x_ref, w_ref, sc_ref, sh_ref, o_ref):
    x = x_ref[...]
    ms = jnp.mean(x * x, axis=-1, keepdims=True)
    y = x * lax.rsqrt(ms + EPS) * w_ref[...]
    o_ref[...] = (y * (1.0 + sc_ref[...]) + sh_ref[...]).astype(o_ref.dtype)


def norm_modulate(x, w, mod, shift_idx, scale_idx, *, ts=512):
    B, S, D = x.shape
    ts = min(ts, S)
    return pl.pallas_call(
        _normmod_kernel,
        grid=(B, S // ts),
        in_specs=[
            pl.BlockSpec((None, ts, D), lambda b, i: (b, i, 0)),
            pl.BlockSpec((1, D), lambda b, i: (0, 0)),
            pl.BlockSpec((None, None, 1, D), lambda b, i: (b, scale_idx, 0, 0)),
            pl.BlockSpec((None, None, 1, D), lambda b, i: (b, shift_idx, 0, 0)),
        ],
        out_specs=pl.BlockSpec((None, ts, D), lambda b, i: (b, i, 0)),
        out_shape=jax.ShapeDtypeStruct((B, S, D), jnp.bfloat16),
        compiler_params=_cparams(2),
        name="norm_modulate",
    )(x, w.reshape(1, D), mod, mod)


def _mm_grouped_kernel(a_ref, w_ref, o_ref):
    acc = jnp.dot(a_ref[...], w_ref[...], preferred_element_type=jnp.float32)
    for g in range(o_ref.shape[0]):
        o_ref[g] = acc[:, g * LANES:(g + 1) * LANES].astype(o_ref.dtype)


def matmul_grouped(a, w, layer, out_dtype, *, tm, tn):
    B, S, K = a.shape
    N = w.shape[2]
    tm, tn = min(tm, S), min(tn, N)
    return pl.pallas_call(
        _mm_grouped_kernel,
        grid=(B, S // tm, N // tn),
        in_specs=[
            pl.BlockSpec((None, tm, K), lambda b, i, j: (b, i, 0)),
            pl.BlockSpec((None, K, tn), lambda b, i, j: (layer, 0, j)),
        ],
        out_specs=pl.BlockSpec((None, tn // LANES, tm, LANES), lambda b, i, j: (b, j, i, 0)),
        out_shape=jax.ShapeDtypeStruct((B, N // LANES, S, LANES), out_dtype),
        compiler_params=_cparams(3),
        name="matmul_grouped",
    )(a, w)


def _mm_res_kernel(*refs, n_lhs):
    a_refs = refs[:n_lhs]
    w_ref, x_ref, g_ref, o_ref = refs[n_lhs:]
    acc = None
    k0 = 0
    for a_ref in a_refs:
        kw = a_ref.shape[-1]
        part = jnp.dot(a_ref[...], w_ref[k0:k0 + kw, :], preferred_element_type=jnp.float32)
        acc = part if acc is None else acc + part
        k0 += kw
    o_ref[...] = x_ref[...] + g_ref[...] * acc


def matmul_residual(lhs_parts, w, layer, x, mod, gate_idx, *, tm, tn):
    B, S, N = x.shape
    K = w.shape[1]
    tm, tn = min(tm, S), min(tn, N)
    n_lhs = len(lhs_parts)
    in_specs = [pl.BlockSpec((None, tm, a.shape[-1]), lambda b, i, j: (b, i, 0)) for a in lhs_parts]
    in_specs += [
        pl.BlockSpec((None, K, tn), lambda b, i, j: (layer, 0, j)),
        pl.BlockSpec((None, tm, tn), lambda b, i, j: (b, i, j)),
        pl.BlockSpec((None, None, 1, tn), lambda b, i, j: (b, gate_idx, 0, j)),
    ]
    return pl.pallas_call(
        functools.partial(_mm_res_kernel, n_lhs=n_lhs),
        grid=(B, S // tm, N // tn),
        in_specs=in_specs,
        out_specs=pl.BlockSpec((None, tm, tn), lambda b, i, j: (b, i, j)),
        out_shape=jax.ShapeDtypeStruct((B, S, N), jnp.float32),
        compiler_params=_cparams(3),
        name="matmul_residual",
    )(*lhs_parts, w, x, mod)


def _convgate_kernel(uv_ref, uvp_ref, uvn_ref, ug_ref, ugp_ref, ugn_ref, wv_ref, wg_ref, bv_ref, bg_ref, o_ref):
    i = pl.program_id(1)
    last = pl.num_programs(1) - 1
    gc, ts, _ = uv_ref.shape
    rc = min(CONV_ROWS, ts)
    n_chunk = ts // rc
    row = lax.broadcasted_iota(jnp.int32, (rc, LANES), 0)

    def conv(g, c, cur_ref, prev_ref, next_ref, w_ref, b_ref):
        lanes = slice(g * LANES, (g + 1) * LANES)
        r0 = c * rc
        cur = cur_ref[g, r0:r0 + rc, :]
        if c == 0:
            prow = jnp.where(i > 0, prev_ref[g][SUBLANES - 1:SUBLANES, :], 0.0)
            up = jnp.where(row == 0, prow, pltpu.roll(cur, 1, 0))
        else:
            up = cur_ref[g, r0 - 1:r0 - 1 + rc, :]
        if c == n_chunk - 1:
            nrow = jnp.where(i < last, next_ref[g][0:1, :], 0.0)
            dn = jnp.where(row == rc - 1, nrow, pltpu.roll(cur, rc - 1, 0))
        else:
            dn = cur_ref[g, r0 + 1:r0 + 1 + rc, :]
        return up * w_ref[0:1, lanes] + cur * w_ref[1:2, lanes] + dn * w_ref[2:3, lanes] + b_ref[:, lanes]

    for g in range(gc):
        for c in range(n_chunk):
            val = conv(g, c, uv_ref, uvp_ref, uvn_ref, wv_ref, bv_ref)
            gate = conv(g, c, ug_ref, ugp_ref, ugn_ref, wg_ref, bg_ref)
            o_ref[c * rc:(c + 1) * rc, g * LANES:(g + 1) * LANES] = (_silu(gate) * val).astype(o_ref.dtype)


def conv_gate(u, conv_w, conv_b, *, ts=CONV_TILE_ROWS, gc=2):
    B, G2, S, _ = u.shape
    nf = G2 // 2
    F = nf * LANES
    ts = min(ts, S)
    nj = nf // gc
    halo_per_tile = ts // SUBLANES

    def cur(off):
        return pl.BlockSpec((None, gc, ts, LANES), lambda b, i, j: (b, off + j, i, 0))

    def prev(off):
        return pl.BlockSpec((None, gc, SUBLANES, LANES),
                            lambda b, i, j: (b, off + j, jnp.maximum(i * halo_per_tile - 1, 0), 0))

    def nxt(off):
        return pl.BlockSpec((None, gc, SUBLANES, LANES),
                            lambda b, i, j: (b, off + j, jnp.minimum((i + 1) * halo_per_tile, S // SUBLANES - 1), 0))

    def par(rows, off):
        return pl.BlockSpec((rows, gc * LANES), lambda b, i, j: (0, off + j))

    cb = conv_b.reshape(1, 2 * F)
    return pl.pallas_call(
        _convgate_kernel,
        grid=(B, S // ts, nj),
        in_specs=[cur(0), prev(0), nxt(0), cur(nj), prev(nj), nxt(nj), par(3, 0), par(3, nj), par(1, 0), par(1, nj)],
        out_specs=pl.BlockSpec((None, ts, gc * LANES), lambda b, i, j: (b, i, j)),
        out_shape=jax.ShapeDtypeStruct((B, S, F), jnp.bfloat16),
        compiler_params=_cparams(3),
        name="conv_gate",
    )(u, u, u, u, u, u, conv_w, conv_w, cb, cb)


def _hgrn_kernel(*refs, reverse, T):
    if reverse:
        (xq_ref, xf_ref, xi_ref, xg_ref, of_ref, llb_ref, l1m_ref, oml_ref, nw_ref, o_ref,
         st_ref, oin_ref, q_ref, k_ref, b_ref) = refs
    else:
        xq_ref, xf_ref, xi_ref, llb_ref, l1m_ref, oml_ref, o_ref, st_ref, oin_ref, q_ref, k_ref, b_ref = refs
    C = CHUNK
    nC = T // C
    half = C // 2
    quarter = C // 4

    @pl.when(pl.program_id(2) == 0)
    def _():
        st_ref[...] = jnp.zeros_like(st_ref)

    xf = xf_ref[...]
    llb, l1m, oml = llb_ref[...], l1m_ref[...], oml_ref[...]
    q = _silu(xq_ref[...])
    v16 = xi_ref[...].astype(jnp.bfloat16)
    e = jnp.exp(-jnp.abs(xf))
    y = l1m + jnp.minimum(xf, 0.0) - jnp.log(1.0 + e)
    logf = jnp.maximum(llb, y) + jnp.log(1.0 + jnp.exp(-jnp.abs(llb - y)))
    kk = oml * jnp.where(xf >= 0.0, e, 1.0) / (1.0 + e)
    span = jnp.sum((-logf).reshape(T // quarter, quarter, LANES), axis=1)
    exact_needed = jnp.max(span) > SAFE_SPAN

    def rows(x, c):
        return x[c * C:(c + 1) * C, :]

    def b_last(bc):
        return bc[0:1, :] if reverse else bc[C - 1:C, :]

    ti = lax.broadcasted_iota(jnp.int32, (C, C), 0)
    si = lax.broadcasted_iota(jnp.int32, (C, C), 1)
    causal = (si >= ti) if reverse else (si <= ti)

    part_hi = logf.astype(jnp.bfloat16)
    rest = logf - part_hi.astype(jnp.float32)
    part_mid = rest.astype(jnp.bfloat16)
    part_lo = (rest - part_mid.astype(jnp.float32)).astype(jnp.bfloat16)
    parts = jnp.concatenate([part_hi, part_mid, part_lo], axis=1)
    tri = jnp.where(causal, 1.0, 0.0).astype(jnp.bfloat16)
    sums = [jnp.dot(tri, rows(parts, c), preferred_element_type=jnp.float32) for c in range(nC)]
    b = jnp.concatenate([d[:, :LANES] + d[:, LANES:2 * LANES] + d[:, 2 * LANES:] for d in sums], axis=0)

    first = lax.broadcasted_iota(jnp.int32, (C, LANES), 0) < half
    q_side = first if reverse else jnp.logical_not(first)
    m0, m1, bnd = (quarter, half + quarter, half) if reverse else (quarter - 1, half + quarter - 1, half - 1)
    zero = jnp.zeros((C, LANES), jnp.bfloat16)

    def intra_factored(c, bc):
        mid = jnp.where(first, bc[m0:m0 + 1, :], bc[m1:m1 + 1, :])
        qa = (rows(q, c) * jnp.exp(bc - mid)).astype(jnp.bfloat16)
        ka = (rows(kk, c) * jnp.exp(mid - bc)).astype(jnp.bfloat16)
        eb = jnp.exp(-jnp.abs(bc - bc[bnd:bnd + 1, :]))
        qb = (rows(q, c) * eb).astype(jnp.bfloat16)
        kb = (rows(kk, c) * eb).astype(jnp.bfloat16)
        qcat = jnp.concatenate([jnp.where(first, qa, zero), jnp.where(first, zero, qa),
                                jnp.where(q_side, qb, zero)], axis=1)
        kcat = jnp.concatenate([jnp.where(first, ka, zero), jnp.where(first, zero, ka),
                                jnp.where(q_side, zero, kb)], axis=1)
        s = lax.dot_general(qcat, kcat, _NT, preferred_element_type=jnp.float32)
        s = jnp.where(causal, s, 0.0).astype(jnp.bfloat16)
        return jnp.dot(s, rows(v16, c), preferred_element_type=jnp.float32)

    def emit(rsl, o):
        if reverse:
            tot = of_ref[rsl, :] + o
            ms = jnp.mean(tot * tot, axis=-1, keepdims=True)
            yn = tot * lax.rsqrt(ms + EPS) * nw_ref[...]
            o_ref[rsl, :] = (yn * _silu(xg_ref[rsl, :])).astype(o_ref.dtype)
        else:
            o_ref[rsl, :] = o

    st = st_ref[...]
    for c in (range(nC - 1, -1, -1) if reverse else range(nC)):
        rsl = slice(c * C, (c + 1) * C)
        bc = rows(b, c)
        bl = b_last(bc)
        qa = (rows(q, c) * jnp.exp(bc)).astype(jnp.bfloat16)
        o_inter = lax.dot_general(qa, st.astype(jnp.bfloat16), _NT, preferred_element_type=jnp.float32)
        oin_ref[rsl, :] = o_inter
        emit(rsl, intra_factored(c, bc) + o_inter)
        kl = (rows(kk, c) * jnp.exp(bl - bc)).astype(jnp.bfloat16)
        st = st * jnp.exp(bl) + lax.dot_general(rows(v16, c), kl, _TN, preferred_element_type=jnp.float32)
    st_ref[...] = st

    @pl.when(exact_needed)
    def _():
        q_ref[...] = q
        k_ref[...] = kk
        b_ref[...] = b
        row = lax.broadcasted_iota(jnp.int32, (C, LANES), 0)

        def per_chunk(c, carry):
            rsl = pl.ds(pl.multiple_of(c * C, C), C)
            qc = q_ref[rsl, :]
            kc = k_ref[rsl, :]
            bc = b_ref[rsl, :]
            vc = xi_ref[rsl, :]

            def pair(d, acc):
                sh = ((C - d) % C) if reverse else d
                valid = (row < C - d) if reverse else (row >= d)
                dec = jnp.exp(jnp.where(valid, bc - pltpu.roll(bc, sh, 0), -jnp.inf))
                w = jnp.sum(qc * pltpu.roll(kc, sh, 0) * dec, axis=-1, keepdims=True)
                return acc + w * pltpu.roll(vc, sh, 0)

            oi = lax.fori_loop(0, C, pair, jnp.zeros((C, LANES), jnp.float32))
            emit(rsl, oi + oin_ref[rsl, :])
            return carry

        lax.fori_loop(0, nC, per_chunk, 0)


def hgrn_pass(zg, gate_params, *, reverse, o_fwd=None, norm_w=None, T=HGRN_BLOCK):
    B, _, S, _ = zg.shape
    T = min(T, S)
    nblk = S // T
    H = HA_HEADS

    def seq(c):
        return (nblk - 1 - c) if reverse else c

    def zspec(section):
        return pl.BlockSpec((None, None, T, LANES), lambda b, h, c: (b, section * H + h, seq(c), 0))

    hspec = pl.BlockSpec((None, None, T, LANES), lambda b, h, c: (b, h, seq(c), 0))
    pspec = pl.BlockSpec((1, LANES), lambda b, h, c: (0, h))
    llb, l1m, oml = gate_params
    if reverse:
        in_specs = [zspec(0), zspec(2), zspec(3), zspec(4), hspec,
                    pspec, pspec, pspec, pl.BlockSpec((1, LANES), lambda b, h, c: (0, 0))]
        args = (zg, zg, zg, zg, o_fwd, llb, l1m, oml, norm_w.reshape(1, LANES))
        out_spec = pl.BlockSpec((None, T, LANES), lambda b, h, c: (b, seq(c), h))
        out_shape = jax.ShapeDtypeStruct((B, S, HA_WIDTH), jnp.bfloat16)
    else:
        in_specs = [zspec(0), zspec(1), zspec(3), pspec, pspec, pspec]
        args = (zg, zg, zg, llb, l1m, oml)
        out_spec = hspec
        out_shape = jax.ShapeDtypeStruct((B, H, S, LANES), jnp.float32)
    blk = pltpu.VMEM((T, LANES), jnp.float32)
    return pl.pallas_call(
        functools.partial(_hgrn_kernel, reverse=reverse, T=T),
        grid=(B, H, nblk),
        in_specs=in_specs,
        out_specs=out_spec,
        out_shape=out_shape,
        scratch_shapes=[pltpu.VMEM((HA_HEAD, HA_HEAD), jnp.float32), blk, blk, blk, blk],
        compiler_params=_cparams(3),
        name="hgrn_bwd" if reverse else "hgrn_fwd",
    )(*args)


def _qknorm_kernel(q_ref, k_ref, v_ref, qw_ref, kw_ref, qo_ref, ko_ref, vo_ref, *, q_scale):
    qw = qw_ref[...] * q_scale
    kw = kw_ref[...]
    for g in range(q_ref.shape[0]):
        h, c = divmod(g, 2)
        sl = slice(c * DB_HEAD, (c + 1) * DB_HEAD)
        for x_ref, w, o_ref in ((q_ref, qw, qo_ref), (k_ref, kw, ko_ref)):
            x = x_ref[g]
            ms = jnp.mean(x * x, axis=-1, keepdims=True)
            o_ref[h, :, sl] = (x * lax.rsqrt(ms + EPS) * w).astype(o_ref.dtype)
        vo_ref[h, :, sl] = v_ref[g].astype(vo_ref.dtype)


def qk_norm(zg, q_norm_w, k_norm_w, *, ts=512):
    B, _, S, _ = zg.shape
    ts = min(ts, S)
    ng = DB_WIDTH // LANES
    first = 5 * HA_WIDTH // DB_WIDTH

    def zspec(sec):
        return pl.BlockSpec((None, ng, ts, LANES), lambda b, i: (b, first + sec, i, 0))

    wspec = pl.BlockSpec((1, DB_HEAD), lambda b, i: (0, 0))
    ospec = pl.BlockSpec((None, DB_HEADS, ts, 2 * DB_HEAD), lambda b, i: (b, 0, i, 0))
    oshape = jax.ShapeDtypeStruct((B, DB_HEADS, S, 2 * DB_HEAD), jnp.bfloat16)
    return pl.pallas_call(
        functools.partial(_qknorm_kernel, q_scale=DB_HEAD ** -0.5 * LOG2E),
        grid=(B, S // ts),
        in_specs=[zspec(0), zspec(1), zspec(2), wspec, wspec],
        out_specs=(ospec, ospec, ospec),
        out_shape=(oshape, oshape, oshape),
        compiler_params=_cparams(2),
        name="qk_norm",
    )(zg, zg, zg, q_norm_w.reshape(1, DB_HEAD), k_norm_w.reshape(1, DB_HEAD))


BIAS_TILES = 5


def _bias_kernel(rb_ref, o_ref, *, t):
    h = pl.program_id(0)
    offset = (pl.program_id(1) - BIAS_TILES // 2) * t
    qi = lax.broadcasted_iota(jnp.int32, (t, t), 0)
    kj = lax.broadcasted_iota(jnp.int32, (t, t), 1)
    rel = kj - qi + offset
    n = jnp.abs(rel)
    half = NUM_BUCKETS // 2

    def side(base):
        val = jnp.full((t, t), rb_ref[base + half - 1, h], jnp.float32)
        for c in range(half - 2, -1, -1):
            val = jnp.where(n < BUCKET_START[c + 1], rb_ref[base + c, h], val)
        return val

    o_ref[...] = jnp.where(rel > 0, side(half), side(0)) * LOG2E


def rel_bias_tiles(rel_bias, t):
    assert t >= 128
    return pl.pallas_call(
        functools.partial(_bias_kernel, t=t),
        grid=(DB_HEADS, BIAS_TILES),
        in_specs=[pl.BlockSpec(memory_space=pltpu.SMEM)],
        out_specs=pl.BlockSpec((None, None, t, t), lambda h, d: (h, d, 0, 0)),
        out_shape=jax.ShapeDtypeStruct((DB_HEADS, BIAS_TILES, t, t), jnp.float32),
        compiler_params=_cparams(2),
        name="rel_bias_tiles",
    )(rel_bias)


def _attn_kernel(q_ref, k_ref, v_ref, bias_ref, lp_ref, sw_ref, o_ref,
                 m_ref, l_ref, acc_ref, m0_ref, l0_ref, acc0_ref, s_a, s_b, p_a, p_b, al_a, al_b, *, lambda_init):
    qb = pl.program_id(2)
    kb = pl.program_id(3)
    t = q_ref.shape[0]
    n_sub = k_ref.shape[0] // t
    rc = min(ATTN_ROWS, t)
    s_bufs, p_bufs, al_bufs = (s_a, s_b), (p_a, p_b), (al_a, al_b)

    @pl.when(kb == 0)
    def _():
        m_ref[...] = jnp.zeros_like(m_ref)
        l_ref[...] = jnp.zeros_like(l_ref)
        acc_ref[...] = jnp.zeros_like(acc_ref)

    m0_ref[...] = m_ref[...]
    l0_ref[...] = l_ref[...]
    acc0_ref[...] = acc_ref[...]

    def logits_into(j, c):
        s_bufs[j % 2][c] = lax.dot_general(q_ref[:, c * DB_HEAD:(c + 1) * DB_HEAD],
                                           k_ref[j * t:(j + 1) * t, c * DB_HEAD:(c + 1) * DB_HEAD],
                                           _NT, preferred_element_type=jnp.float32)

    def widen(x, width):
        return jnp.tile(x, (1, width // LANES))

    def sweep(safe):
        excess = jnp.zeros((rc, LANES), jnp.float32)
        for c in range(2):
            logits_into(0, c)
        for j in range(n_sub):
            s_buf, p_buf, al_buf = s_bufs[j % 2], p_bufs[j % 2], al_bufs[j % 2]
            tile_idx = jnp.clip(kb * n_sub + j - qb, -2, 2) + 2
            for c in range(2):
                if j + 1 < n_sub:
                    logits_into(j + 1, c)
                for i in range(t // rc):
                    rows = slice(i * rc, (i + 1) * rc)
                    crows = slice(c * t + i * rc, c * t + (i + 1) * rc)
                    s = s_buf[c, rows, :] + bias_ref[tile_idx, rows, :]
                    smax = jnp.max(s, axis=-1, keepdims=True)
                    r = m_ref[c, rows, :]
                    m_new = jnp.maximum(r, smax)
                    alpha = jnp.exp2(r - m_new)
                    if safe:
                        p = jnp.exp2(s - widen(m_new, t))
                        l_ref[c, rows, :] = alpha * l_ref[c, rows, :] + jnp.sum(p, axis=-1, keepdims=True)
                    else:
                        p = jnp.exp2(s - widen(r, t))
                        l_ref[c, rows, :] = alpha * (l_ref[c, rows, :] + jnp.sum(p, axis=-1, keepdims=True))
                        gap = smax - r
                        if j == 0:
                            gap = jnp.where(kb == 0, jnp.abs(gap), gap)
                        excess = jnp.maximum(excess, gap)
                    m_ref[c, rows, :] = m_new
                    p_buf[crows, :] = p.astype(jnp.bfloat16)
                    al_buf[crows, :] = alpha
                mrows = slice(c * t, (c + 1) * t)
                pv = jnp.dot(p_buf[mrows, :], v_ref[j * t:(j + 1) * t, :], preferred_element_type=jnp.float32)
                a2 = widen(al_buf[mrows, :], 2 * LANES)
                acc_ref[mrows, :] = (a2 * acc_ref[mrows, :] + pv) if safe else (a2 * (acc_ref[mrows, :] + pv))
        return excess

    redo = jnp.max(sweep(safe=False)) > SAFE_LOG2_GAP

    @pl.when(redo)
    def _():
        m_ref[...] = m0_ref[...] + jnp.where(kb == 0, -jnp.inf, 0.0)
        l_ref[...] = l0_ref[...]
        acc_ref[...] = acc0_ref[...]
        sweep(safe=True)

    @pl.when(kb == pl.num_programs(3) - 1)
    def _():
        lp = lp_ref[...]
        lam = (jnp.exp(jnp.sum(lp[0:1, :] * lp[1:2, :], axis=-1, keepdims=True))
               - jnp.exp(jnp.sum(lp[2:3, :] * lp[3:4, :], axis=-1, keepdims=True)) + lambda_init)
        o = (acc_ref[0:t, :] / jnp.tile(l_ref[0], (1, 2))
             - lam * (acc_ref[t:2 * t, :] / jnp.tile(l_ref[1], (1, 2))))
        ms = jnp.mean(o * o, axis=-1, keepdims=True)
        o_ref[...] = (o * lax.rsqrt(ms + EPS) * sw_ref[...] * (1.0 - lambda_init)).astype(o_ref.dtype)


def diff_attention(qn, kn, vb, bias_tiles, lam_params, subln_w, lambda_init):
    B, H, S, W = qn.shape
    t = bias_tiles.shape[-1]
    tk = min(ATTN_KEYS_PER_STEP, S)
    stat = pltpu.VMEM((2, t, LANES), jnp.float32)
    accs = pltpu.VMEM((2 * t, W), jnp.float32)
    return pl.pallas_call(
        functools.partial(_attn_kernel, lambda_init=lambda_init),
        grid=(B, H, S // t, S // tk),
        in_specs=[
            pl.BlockSpec((None, None, t, W), lambda b, h, i, j: (b, h, i, 0)),
            pl.BlockSpec((None, None, tk, W), lambda b, h, i, j: (b, h, j, 0)),
            pl.BlockSpec((None, None, tk, W), lambda b, h, i, j: (b, h, j, 0)),
            pl.BlockSpec((None, BIAS_TILES, t, t), lambda b, h, i, j: (h, 0, 0, 0)),
            pl.BlockSpec((4, DB_HEAD), lambda b, h, i, j: (0, 0)),
            pl.BlockSpec((1, W), lambda b, h, i, j: (0, 0)),
        ],
        out_specs=pl.BlockSpec((None, t, W), lambda b, h, i, j: (b, i, h)),
        out_shape=jax.ShapeDtypeStruct((B, S, DB_WIDTH), jnp.bfloat16),
        scratch_shapes=[stat, stat, accs, stat, stat, accs,
                        pltpu.VMEM((2, t, t), jnp.float32), pltpu.VMEM((2, t, t), jnp.float32),
                        pltpu.VMEM((2 * t, t), jnp.bfloat16), pltpu.VMEM((2 * t, t), jnp.bfloat16),
                        pltpu.VMEM((2 * t, LANES), jnp.float32), pltpu.VMEM((2 * t, LANES), jnp.float32)],
        compiler_params=_cparams(4),
        name="diff_attention",
    )(qn, kn, vb, bias_tiles, lam_params, subln_w.reshape(1, W))


def _trunk(x, mod, layers, weights, bias_tiles):
    for l, p in enumerate(layers):
        m = mod[l]
        lambda_init = 0.8 - 0.6 * math.exp(-0.3 * l)
        h = norm_modulate(x, p["norm_attn_w"], m, 0, 1)
        zg = matmul_grouped(h, weights["w_in"], l, jnp.float32, tm=W_IN_TILE[0], tn=W_IN_TILE[1])
        o_fwd = hgrn_pass(zg, p["gates_fwd"], reverse=False)
        out_a = hgrn_pass(zg, p["gates_bwd"], reverse=True, o_fwd=o_fwd, norm_w=p["hgrn_norm_w"])
        qn, kn, vb = qk_norm(zg, p["q_norm_w"], p["k_norm_w"])
        out_b = diff_attention(qn, kn, vb, bias_tiles, p["diff_lambda"], p["diff_subln_w"], lambda_init)
        x = matmul_residual([out_a, out_b], weights["w_out"], l, x, m, 2, tm=W_OUT_TILE[0], tn=W_OUT_TILE[1])
        h = norm_modulate(x, p["norm_ffn_w"], m, 3, 4)
        ug = matmul_grouped(h, weights["w_up"], l, jnp.float32, tm=W_UP_TILE[0], tn=W_UP_TILE[1])
        act = conv_gate(ug, p["conv_w"], p["conv_b"])
        x = matmul_residual([act], weights["w_down"], l, x, m, 5, tm=W_DOWN_TILE[0], tn=W_DOWN_TILE[1])
    return x


def kernel(x_prompt, x_sample, c_prompt, c_sample, w_ada, b_ada, norm_attn_w, w_in, hgrn_lb, hgrn_norm_w,
           q_norm_w, k_norm_w, diff_lambda, diff_subln_w, rel_bias, w_out, norm_ffn_w, w_up, conv_w, conv_b,
           w_down):
    L = w_ada.shape[0]
    D = x_prompt.shape[-1]
    Bp, Bs = c_prompt.shape[0], c_sample.shape[0]
    rows = -(-(Bp + Bs) // SUBLANES) * SUBLANES
    c_all = jnp.zeros((rows, D), jnp.float32).at[:Bp].set(c_prompt).at[Bp:Bp + Bs].set(c_sample)
    mod = adaln_mod(c_all, w_ada, b_ada)
    mod_p = mod[:, :Bp].reshape(L, Bp, 6, 1, D)
    mod_s = mod[:, Bp:Bp + Bs].reshape(L, Bs, 6, 1, D)

    lb_cum = jnp.cumsum(jax.nn.softmax(hgrn_lb.astype(jnp.float32), axis=0), axis=0)
    lb_all = lb_cum - lb_cum[0:1]
    log_lb, log_1m_lb, one_m_lb = jnp.log(lb_all), jnp.log1p(-lb_all), 1.0 - lb_all

    layers = []
    for l in range(L):
        def gates(d):
            return (log_lb[l, d][None], log_1m_lb[l, d][None], one_m_lb[l, d][None])
        layers.append(dict(
            norm_attn_w=norm_attn_w[l], gates_fwd=gates(0), gates_bwd=gates(1), hgrn_norm_w=hgrn_norm_w[l],
            q_norm_w=q_norm_w[l], k_norm_w=k_norm_w[l], diff_lambda=diff_lambda[l].astype(jnp.float32),
            diff_subln_w=diff_subln_w[l], norm_ffn_w=norm_ffn_w[l], conv_w=conv_w[l], conv_b=conv_b[l]))
    weights = dict(w_in=w_in.astype(jnp.bfloat16), w_out=w_out.astype(jnp.bfloat16),
                   w_up=w_up.astype(jnp.bfloat16), w_down=w_down.astype(jnp.bfloat16))
    rel_bias = rel_bias.astype(jnp.float32)
    tiles = {}
    outs = []
    for x, m in ((x_prompt, mod_p), (x_sample, mod_s)):
        S = x.shape[1]
        tile = min(ATTN_TILE, S)
        if tile not in tiles:
            tiles[tile] = rel_bias_tiles(rel_bias, tile)
        outs.append(_trunk(x, m, layers, weights, tiles[tile]))
    return tuple(outs)
```

```python
import functools
import math

import jax
import jax.numpy as jnp
from jax import lax
from jax.experimental import pallas as pl
from jax.experimental.pallas import tpu as pltpu

D_MODEL = 4096
HA_HEADS = 16
HA_HEAD = 128
HA_WIDTH = HA_HEADS * HA_HEAD
DB_HEADS = 8
DB_HEAD = 128
DB_WIDTH = DB_HEADS * 2 * DB_HEAD
IN_COLS = 5 * HA_WIDTH + 3 * DB_WIDTH
CHUNK = 64
NUM_BUCKETS = 32
EPS = 1e-6
BUCKET_START = (0, 1, 2, 3, 4, 5, 6, 7, 8, 12, 16, 23, 32, 46, 64, 91)

LANES = 128
SUBLANES = 8
V7X_VMEM_BYTES = 64 * 1024 * 1024
VMEM_LIMIT = V7X_VMEM_BYTES - 8 * 1024 * 1024

W_IN_TILE = (1024, 1024)
W_OUT_TILE = (1024, 1024)
W_UP_TILE = (2048, 512)
W_DOWN_TILE = (512, 512)
HGRN_BLOCK = 2048
CONV_TILE_ROWS = 2048
CONV_ROWS = 64
ATTN_TILE = 512
ATTN_KEYS_PER_STEP = 4096
ATTN_ROWS = 32

LOG2E = 1.0 / math.log(2.0)
SAFE_SPAN = 80.0
SAFE_LOG2_GAP = 60.0

_NT = (((1,), (1,)), ((), ()))
_TN = (((0,), (0,)), ((), ()))


def _cparams(n_axes):
    return pltpu.CompilerParams(
        dimension_semantics=("arbitrary",) * n_axes, vmem_limit_bytes=VMEM_LIMIT
    )


def _silu(x):
    return x * jax.nn.sigmoid(x)


def _ada_kernel(c_ref, w_ref, b_ref, o_ref):
    @pl.when(pl.program_id(2) == 0)
    def _():
        o_ref[...] = jnp.broadcast_to(b_ref[...], o_ref.shape)

    a = _silu(c_ref[...]).astype(jnp.bfloat16)
    o_ref[...] += jnp.dot(a, w_ref[...].astype(jnp.bfloat16), preferred_element_type=jnp.float32)


def adaln_mod(c_all, w_ada, b_ada, *, tk=1024, tn=2048):
    R, D = c_all.shape
    L, _, N = w_ada.shape
    return pl.pallas_call(
        _ada_kernel,
        grid=(L, N // tn, D // tk),
        in_specs=[
            pl.BlockSpec((R, tk), lambda l, j, k: (0, k)),
            pl.BlockSpec((None, tk, tn), lambda l, j, k: (l, k, j)),
            pl.BlockSpec((None, 1, tn), lambda l, j, k: (l, 0, j)),
        ],
        out_specs=pl.BlockSpec((None, R, tn), lambda l, j, k: (l, 0, j)),
        out_shape=jax.ShapeDtypeStruct((L, R, N), jnp.float32),
        compiler_params=_cparams(3),
        name="adaln_mod",
    )(c_all, w_ada, b_ada.reshape(L, 1, N))


def _normmod_kernel(x_ref, w_ref, sc_ref, sh_ref, o_ref):
    x = x_ref[...]
    ms = jnp.mean(x * x, axis=-1, keepdims=True)
    y = x * lax.rsqrt(ms + EPS) * w_ref[...]
    o_ref[...] = (y * (1.0 + sc_ref[...]) + sh_ref[...]).astype(o_ref.dtype)


def norm_modulate(x, w, mod, shift_idx, scale_idx, *, ts=512):
    B, S, D = x.shape
    ts = min(ts, S)
    return pl.pallas_call(
        _normmod_kernel,
        grid=(B, S // ts),
        in_specs=[
            pl.BlockSpec((None, ts, D), lambda b, i: (b, i, 0)),
            pl.BlockSpec((1, D), lambda b, i: (0, 0)),
            pl.BlockSpec((None, None, 1, D), lambda b, i: (b, scale_idx, 0, 0)),
            pl.BlockSpec((None, None, 1, D), lambda b, i: (b, shift_idx, 0, 0)),
        ],
        out_specs=pl.BlockSpec((None, ts, D), lambda b, i: (b, i, 0)),
        out_shape=jax.ShapeDtypeStruct((B, S, D), jnp.bfloat16),
        compiler_params=_cparams(2),
        name="norm_modulate",
    )(x, w.reshape(1, D), mod, mod)


def _mm_grouped_kernel(a_ref, w_ref, o_ref):
    acc = jnp.dot(a_ref[...], w_ref[...], preferred_element_type=jnp.float32)
    for g in range(o_ref.shape[0]):
        o_ref[g] = acc[:, g * LANES:(g + 1) * LANES].astype(o_ref.dtype)


def matmul_grouped(a, w, layer, out_dtype, *, tm, tn):
    B, S, K = a.shape
    N = w.shape[2]
    tm, tn = min(tm, S), min(tn, N)
    return pl.pallas_call(
        _mm_grouped_kernel,
        grid=(B, S // tm, N // tn),
        in_specs=[
            pl.BlockSpec((None, tm, K), lambda b, i, j: (b, i, 0)),
            pl.BlockSpec((None, K, tn), lambda b, i, j: (layer, 0, j)),
        ],
        out_specs=pl.BlockSpec((None, tn // LANES, tm, LANES), lambda b, i, j: (b, j, i, 0)),
        out_shape=jax.ShapeDtypeStruct((B, N // LANES, S, LANES), out_dtype),
        compiler_params=_cparams(3),
        name="matmul_grouped",
    )(a, w)


def _mm_res_kernel(*refs, n_lhs):
    a_refs = refs[:n_lhs]
    w_ref, x_ref, g_ref, o_ref = refs[n_lhs:]
    acc = None
    k0 = 0
    for a_ref in a_refs:
        kw = a_ref.shape[-1]
        part = jnp.dot(a_ref[...], w_ref[k0:k0 + kw, :], preferred_element_type=jnp.float32)
        acc = part if acc is None else acc + part
        k0 += kw
    o_ref[...] = x_ref[...] + g_ref[...] * acc


def matmul_residual(lhs_parts, w, layer, x, mod, gate_idx, *, tm, tn):
    B, S, N = x.shape
    K = w.shape[1]
    tm, tn = min(tm, S), min(tn, N)
    n_lhs = len(lhs_parts)
    in_specs = [pl.BlockSpec((None, tm, a.shape[-1]), lambda b, i, j: (b, i, 0)) for a in lhs_parts]
    in_specs += [
        pl.BlockSpec((None, K, tn), lambda b, i, j: (layer, 0, j)),
        pl.BlockSpec((None, tm, tn), lambda b, i, j: (b, i, j)),
        pl.BlockSpec((None, None, 1, tn), lambda b, i, j: (b, gate_idx, 0, j)),
    ]
    return pl.pallas_call(
        functools.partial(_mm_res_kernel, n_lhs=n_lhs),
        grid=(B, S // tm, N // tn),
        in_specs=in_specs,
        out_specs=pl.BlockSpec((None, tm, tn), lambda b, i, j: (b, i, j)),
        out_shape=jax.ShapeDtypeStruct((B, S, N), jnp.float32),
        compiler_params=_cparams(3),
        name="matmul_residual",
    )(*lhs_parts, w, x, mod)


def _convgate_kernel(uv_ref, uvp_ref, uvn_ref, ug_ref, ugp_ref, ugn_ref, wv_ref, wg_ref, bv_ref, bg_ref, o_ref):
    i = pl.program_id(1)
    last = pl.num_programs(1) - 1
    gc, ts, _ = uv_ref.shape
    rc = min(CONV_ROWS, ts)
    n_chunk = ts // rc
    row = lax.broadcasted_iota(jnp.int32, (rc, LANES), 0)

    def conv(g, c, cur_ref, prev_ref, next_ref, w_ref, b_ref):
        lanes = slice(g * LANES, (g + 1) * LANES)
        r0 = c * rc
        cur = cur_ref[g, r0:r0 + rc, :]
        if c == 0:
            prow = jnp.where(i > 0, prev_ref[g][SUBLANES - 1:SUBLANES, :], 0.0)
            up = jnp.where(row == 0, prow, pltpu.roll(cur, 1, 0))
        else:
            up = cur_ref[g, r0 - 1:r0 - 1 + rc, :]
        if c == n_chunk - 1:
            nrow = jnp.where(i < last, next_ref[g][0:1, :], 0.0)
            dn = jnp.where(row == rc - 1, nrow, pltpu.roll(cur, rc - 1, 0))
        else:
            dn = cur_ref[g, r0 + 1:r0 + 1 + rc, :]
        return up * w_ref[0:1, lanes] + cur * w_ref[1:2, lanes] + dn * w_ref[2:3, lanes] + b_ref[:, lanes]

    for g in range(gc):
        for c in range(n_chunk):
            val = conv(g, c, uv_ref, uvp_ref, uvn_ref, wv_ref, bv_ref)
            gate = conv(g, c, ug_ref, ugp_ref, ugn_ref, wg_ref, bg_ref)
            o_ref[c * rc:(c + 1) * rc, g * LANES:(g + 1) * LANES] = (_silu(gate) * val).astype(o_ref.dtype)


def conv_gate(u, conv_w, conv_b, *, ts=CONV_TILE_ROWS, gc=2):
    B, G2, S, _ = u.shape
    nf = G2 // 2
    F = nf * LANES
    ts = min(ts, S)
    nj = nf // gc
    halo_per_tile = ts // SUBLANES

    def cur(off):
        return pl.BlockSpec((None, gc, ts, LANES), lambda b, i, j: (b, off + j, i, 0))

    def prev(off):
        return pl.BlockSpec((None, gc, SUBLANES, LANES),
                            lambda b, i, j: (b, off + j, jnp.maximum(i * halo_per_tile - 1, 0), 0))

    def nxt(off):
        return pl.BlockSpec((None, gc, SUBLANES, LANES),
                            lambda b, i, j: (b, off + j, jnp.minimum((i + 1) * halo_per_tile, S // SUBLANES - 1), 0))

    def par(rows, off):
        return pl.BlockSpec((rows, gc * LANES), lambda b, i, j: (0, off + j))

    cb = conv_b.reshape(1, 2 * F)
    return pl.pallas_call(
        _convgate_kernel,
        grid=(B, S // ts, nj),
        in_specs=[cur(0), prev(0), nxt(0), cur(nj), prev(nj), nxt(nj), par(3, 0), par(3, nj), par(1, 0), par(1, nj)],
        out_specs=pl.BlockSpec((None, ts, gc * LANES), lambda b, i, j: (b, i, j)),
        out_shape=jax.ShapeDtypeStruct((B, S, F), jnp.bfloat16),
        compiler_params=_cparams(3),
        name="conv_gate",
    )(u, u, u, u, u, u, conv_w, conv_w, cb, cb)


def _hgrn_kernel(*refs, reverse, T):
    if reverse:
        (xq_ref, xf_ref, xi_ref, xg_ref, of_ref, llb_ref, l1m_ref, oml_ref, nw_ref, o_ref,
         st_ref, oin_ref, q_ref, k_ref, b_ref) = refs
    else:
        xq_ref, xf_ref, xi_ref, llb_ref, l1m_ref, oml_ref, o_ref, st_ref, oin_ref, q_ref, k_ref, b_ref = refs
    C = CHUNK
    nC = T // C
    half = C // 2
    quarter = C // 4

    @pl.when(pl.program_id(2) == 0)
    def _():
        st_ref[...] = jnp.zeros_like(st_ref)

    xf = xf_ref[...]
    llb, l1m, oml = llb_ref[...], l1m_ref[...], oml_ref[...]
    q = _silu(xq_ref[...])
    v16 = xi_ref[...].astype(jnp.bfloat16)
    e = jnp.exp(-jnp.abs(xf))
    y = l1m + jnp.minimum(xf, 0.0) - jnp.log(1.0 + e)
    logf = jnp.maximum(llb, y) + jnp.log(1.0 + jnp.exp(-jnp.abs(llb - y)))
    kk = oml * jnp.where(xf >= 0.0, e, 1.0) / (1.0 + e)
    span = jnp.sum((-logf).reshape(T // quarter, quarter, LANES), axis=1)
    exact_needed = jnp.max(span) + jnp.log(jnp.maximum(jnp.max(jnp.abs(q)), 1.0)) > SAFE_SPAN

    def rows(x, c):
        return x[c * C:(c + 1) * C, :]

    def b_last(bc):
        return bc[0:1, :] if reverse else bc[C - 1:C, :]

    ti = lax.broadcasted_iota(jnp.int32, (C, C), 0)
    si = lax.broadcasted_iota(jnp.int32, (C, C), 1)
    causal = (si >= ti) if reverse else (si <= ti)

    part_hi = logf.astype(jnp.bfloat16)
    rest = logf - part_hi.astype(jnp.float32)
    part_mid = rest.astype(jnp.bfloat16)
    part_lo = (rest - part_mid.astype(jnp.float32)).astype(jnp.bfloat16)
    parts = jnp.concatenate([part_hi, part_mid, part_lo], axis=1)
    tri = jnp.where(causal, 1.0, 0.0).astype(jnp.bfloat16)
    sums = [jnp.dot(tri, rows(parts, c), preferred_element_type=jnp.float32) for c in range(nC)]
    b = jnp.concatenate([d[:, :LANES] + d[:, LANES:2 * LANES] + d[:, 2 * LANES:] for d in sums], axis=0)

    first = lax.broadcasted_iota(jnp.int32, (C, LANES), 0) < half
    q_side = first if reverse else jnp.logical_not(first)
    m0, m1, bnd = (quarter, half + quarter, half) if reverse else (quarter - 1, half + quarter - 1, half - 1)
    zero = jnp.zeros((C, LANES), jnp.bfloat16)

    def intra_factored(c, bc):
        mid = jnp.where(first, bc[m0:m0 + 1, :], bc[m1:m1 + 1, :])
        qa = (rows(q, c) * jnp.exp(bc - mid)).astype(jnp.bfloat16)
        ka = (rows(kk, c) * jnp.exp(mid - bc)).astype(jnp.bfloat16)
        eb = jnp.exp(-jnp.abs(bc - bc[bnd:bnd + 1, :]))
        qb = (rows(q, c) * eb).astype(jnp.bfloat16)
        kb = (rows(kk, c) * eb).astype(jnp.bfloat16)
        qcat = jnp.concatenate([jnp.where(first, qa, zero), jnp.where(first, zero, qa),
                                jnp.where(q_side, qb, zero)], axis=1)
        kcat = jnp.concatenate([jnp.where(first, ka, zero), jnp.where(first, zero, ka),
                                jnp.where(q_side, zero, kb)], axis=1)
        s = lax.dot_general(qcat, kcat, _NT, preferred_element_type=jnp.float32)
        s = jnp.where(causal, s, 0.0).astype(jnp.bfloat16)
        return jnp.dot(s, rows(v16, c), preferred_element_type=jnp.float32)

    def emit(rsl, o):
        if reverse:
            tot = of_ref[rsl, :] + o
            ms = jnp.mean(tot * tot, axis=-1, keepdims=True)
            yn = tot * lax.rsqrt(ms + EPS) * nw_ref[...]
            o_ref[rsl, :] = (yn * _silu(xg_ref[rsl, :])).astype(o_ref.dtype)
        else:
            o_ref[rsl, :] = o

    st = st_ref[...]
    for c in (range(nC - 1, -1, -1) if reverse else range(nC)):
        rsl = slice(c * C, (c + 1) * C)
        bc = rows(b, c)
        bl = b_last(bc)
        qa = (rows(q, c) * jnp.exp(bc)).astype(jnp.bfloat16)
        o_inter = lax.dot_general(qa, st.astype(jnp.bfloat16), _NT, preferred_element_type=jnp.float32)
        oin_ref[rsl, :] = o_inter
        emit(rsl, intra_factored(c, bc) + o_inter)
        kl = (rows(kk, c) * jnp.exp(bl - bc)).astype(jnp.bfloat16)
        st = st * jnp.exp(bl) + lax.dot_general(rows(v16, c), kl, _TN, preferred_element_type=jnp.float32)
    st_ref[...] = st

    @pl.when(exact_needed)
    def _():
        q_ref[...] = q
        k_ref[...] = kk
        b_ref[...] = b
        row = lax.broadcasted_iota(jnp.int32, (C, LANES), 0)

        def per_chunk(c, carry):
            rsl = pl.ds(pl.multiple_of(c * C, C), C)
            qc = q_ref[rsl, :]
            kc = k_ref[rsl, :]
            bc = b_ref[rsl, :]
            vc = xi_ref[rsl, :]

            def pair(d, acc):
                sh = ((C - d) % C) if reverse else d
                valid = (row < C - d) if reverse else (row >= d)
                dec = jnp.exp(jnp.where(valid, bc - pltpu.roll(bc, sh, 0), -jnp.inf))
                w = jnp.sum(qc * pltpu.roll(kc, sh, 0) * dec, axis=-1, keepdims=True)
                return acc + w * pltpu.roll(vc, sh, 0)

            oi = lax.fori_loop(0, C, pair, jnp.zeros((C, LANES), jnp.float32))
            emit(rsl, oi + oin_ref[rsl, :])
            return carry

        lax.fori_loop(0, nC, per_chunk, 0)


def hgrn_pass(zg, gate_params, *, reverse, o_fwd=None, norm_w=None, T=HGRN_BLOCK):
    B, _, S, _ = zg.shape
    T = min(T, S)
    nblk = S // T
    H = HA_HEADS

    def seq(c):
        return (nblk - 1 - c) if reverse else c

    def zspec(section):
        return pl.BlockSpec((None, None, T, LANES), lambda b, h, c: (b, section * H + h, seq(c), 0))

    hspec = pl.BlockSpec((None, None, T, LANES), lambda b, h, c: (b, h, seq(c), 0))
    pspec = pl.BlockSpec((1, LANES), lambda b, h, c: (0, h))
    llb, l1m, oml = gate_params
    if reverse:
        in_specs = [zspec(0), zspec(2), zspec(3), zspec(4), hspec,
                    pspec, pspec, pspec, pl.BlockSpec((1, LANES), lambda b, h, c: (0, 0))]
        args = (zg, zg, zg, zg, o_fwd, llb, l1m, oml, norm_w.reshape(1, LANES))
        out_spec = pl.BlockSpec((None, T, LANES), lambda b, h, c: (b, seq(c), h))
        out_shape = jax.ShapeDtypeStruct((B, S, HA_WIDTH), jnp.bfloat16)
    else:
        in_specs = [zspec(0), zspec(1), zspec(3), pspec, pspec, pspec]
        args = (zg, zg, zg, llb, l1m, oml)
        out_spec = hspec
        out_shape = jax.ShapeDtypeStruct((B, H, S, LANES), jnp.float32)
    blk = pltpu.VMEM((T, LANES), jnp.float32)
    return pl.pallas_call(
        functools.partial(_hgrn_kernel, reverse=reverse, T=T),
        grid=(B, H, nblk),
        in_specs=in_specs,
        out_specs=out_spec,
        out_shape=out_shape,
        scratch_shapes=[pltpu.VMEM((HA_HEAD, HA_HEAD), jnp.float32), blk, blk, blk, blk],
        compiler_params=_cparams(3),
        name="hgrn_bwd" if reverse else "hgrn_fwd",
    )(*args)


def _qknorm_kernel(q_ref, k_ref, v_ref, qw_ref, kw_ref, qo_ref, ko_ref, vo_ref, *, q_scale):
    qw = qw_ref[...] * q_scale
    kw = kw_ref[...]
    for g in range(q_ref.shape[0]):
        h, c = divmod(g, 2)
        sl = slice(c * DB_HEAD, (c + 1) * DB_HEAD)
        for x_ref, w, o_ref in ((q_ref, qw, qo_ref), (k_ref, kw, ko_ref)):
            x = x_ref[g]
            ms = jnp.mean(x * x, axis=-1, keepdims=True)
            o_ref[h, :, sl] = (x * lax.rsqrt(ms + EPS) * w).astype(o_ref.dtype)
        vo_ref[h, :, sl] = v_ref[g].astype(vo_ref.dtype)


def qk_norm(zg, q_norm_w, k_norm_w, *, ts=512):
    B, _, S, _ = zg.shape
    ts = min(ts, S)
    ng = DB_WIDTH // LANES
    first = 5 * HA_WIDTH // DB_WIDTH

    def zspec(sec):
        return pl.BlockSpec((None, ng, ts, LANES), lambda b, i: (b, first + sec, i, 0))

    wspec = pl.BlockSpec((1, DB_HEAD), lambda b, i: (0, 0))
    ospec = pl.BlockSpec((None, DB_HEADS, ts, 2 * DB_HEAD), lambda b, i: (b, 0, i, 0))
    oshape = jax.ShapeDtypeStruct((B, DB_HEADS, S, 2 * DB_HEAD), jnp.bfloat16)
    return pl.pallas_call(
        functools.partial(_qknorm_kernel, q_scale=DB_HEAD ** -0.5 * LOG2E),
        grid=(B, S // ts),
        in_specs=[zspec(0), zspec(1), zspec(2), wspec, wspec],
        out_specs=(ospec, ospec, ospec),
        out_shape=(oshape, oshape, oshape),
        compiler_params=_cparams(2),
        name="qk_norm",
    )(zg, zg, zg, q_norm_w.reshape(1, DB_HEAD), k_norm_w.reshape(1, DB_HEAD))


BIAS_TILES = 5


def _bias_kernel(rb_ref, o_ref, *, t):
    h = pl.program_id(0)
    offset = (pl.program_id(1) - BIAS_TILES // 2) * t
    qi = lax.broadcasted_iota(jnp.int32, (t, t), 0)
    kj = lax.broadcasted_iota(jnp.int32, (t, t), 1)
    rel = kj - qi + offset
    n = jnp.abs(rel)
    half = NUM_BUCKETS // 2

    def side(base):
        val = jnp.full((t, t), rb_ref[base + half - 1, h], jnp.float32)
        for c in range(half - 2, -1, -1):
            val = jnp.where(n < BUCKET_START[c + 1], rb_ref[base + c, h], val)
        return val

    o_ref[...] = jnp.where(rel > 0, side(half), side(0)) * LOG2E


def rel_bias_tiles(rel_bias, t):
    assert t >= 128
    return pl.pallas_call(
        functools.partial(_bias_kernel, t=t),
        grid=(DB_HEADS, BIAS_TILES),
        in_specs=[pl.BlockSpec(memory_space=pltpu.SMEM)],
        out_specs=pl.BlockSpec((None, None, t, t), lambda h, d: (h, d, 0, 0)),
        out_shape=jax.ShapeDtypeStruct((DB_HEADS, BIAS_TILES, t, t), jnp.float32),
        compiler_params=_cparams(2),
        name="rel_bias_tiles",
    )(rel_bias)


def _attn_kernel(q_ref, k_ref, v_ref, bias_ref, lp_ref, sw_ref, o_ref,
                 m_ref, l_ref, acc_ref, m0_ref, l0_ref, acc0_ref, s_a, s_b, p_a, p_b, al_a, al_b, *, lambda_init):
    qb = pl.program_id(2)
    kb = pl.program_id(3)
    t = q_ref.shape[0]
    n_sub = k_ref.shape[0] // t
    rc = min(ATTN_ROWS, t)
    s_bufs, p_bufs, al_bufs = (s_a, s_b), (p_a, p_b), (al_a, al_b)

    @pl.when(kb == 0)
    def _():
        m_ref[...] = jnp.zeros_like(m_ref)
        l_ref[...] = jnp.zeros_like(l_ref)
        acc_ref[...] = jnp.zeros_like(acc_ref)

    m0_ref[...] = m_ref[...]
    l0_ref[...] = l_ref[...]
    acc0_ref[...] = acc_ref[...]

    def logits_into(j, c):
        s_bufs[j % 2][c] = lax.dot_general(q_ref[:, c * DB_HEAD:(c + 1) * DB_HEAD],
                                           k_ref[j * t:(j + 1) * t, c * DB_HEAD:(c + 1) * DB_HEAD],
                                           _NT, preferred_element_type=jnp.float32)

    def widen(x, width):
        return jnp.tile(x, (1, width // LANES))

    def sweep(safe):
        excess = jnp.zeros((rc, LANES), jnp.float32)
        for c in range(2):
            logits_into(0, c)
        for j in range(n_sub):
            s_buf, p_buf, al_buf = s_bufs[j % 2], p_bufs[j % 2], al_bufs[j % 2]
            tile_idx = jnp.clip(kb * n_sub + j - qb, -2, 2) + 2
            for c in range(2):
                if j + 1 < n_sub:
                    logits_into(j + 1, c)
                for i in range(t // rc):
                    rows = slice(i * rc, (i + 1) * rc)
                    crows = slice(c * t + i * rc, c * t + (i + 1) * rc)
                    s = s_buf[c, rows, :] + bias_ref[tile_idx, rows, :]
                    smax = jnp.max(s, axis=-1, keepdims=True)
                    r = m_ref[c, rows, :]
                    m_new = jnp.maximum(r, smax)
                    alpha = jnp.exp2(r - m_new)
                    if safe:
                        p = jnp.exp2(s - widen(m_new, t))
                        l_ref[c, rows, :] = alpha * l_ref[c, rows, :] + jnp.sum(p, axis=-1, keepdims=True)
                    else:
                        p = jnp.exp2(s - widen(r, t))
                        l_ref[c, rows, :] = alpha * (l_ref[c, rows, :] + jnp.sum(p, axis=-1, keepdims=True))
                        gap = smax - r
                        if j == 0:
                            gap = jnp.where(kb == 0, jnp.abs(gap), gap)
                        excess = jnp.maximum(excess, gap)
                    m_ref[c, rows, :] = m_new
                    p_buf[crows, :] = p.astype(jnp.bfloat16)
                    al_buf[crows, :] = alpha
                mrows = slice(c * t, (c + 1) * t)
                pv = jnp.dot(p_buf[mrows, :], v_ref[j * t:(j + 1) * t, :], preferred_element_type=jnp.float32)
                a2 = widen(al_buf[mrows, :], 2 * LANES)
                acc_ref[mrows, :] = (a2 * acc_ref[mrows, :] + pv) if safe else (a2 * (acc_ref[mrows, :] + pv))
        return excess

    redo = jnp.max(sweep(safe=False)) > SAFE_LOG2_GAP

    @pl.when(redo)
    def _():
        m_ref[...] = m0_ref[...] + jnp.where(kb == 0, -jnp.inf, 0.0)
        l_ref[...] = l0_ref[...]
        acc_ref[...] = acc0_ref[...]
        sweep(safe=True)

    @pl.when(kb == pl.num_programs(3) - 1)
    def _():
        lp = lp_ref[...]
        lam = (jnp.exp(jnp.sum(lp[0:1, :] * lp[1:2, :], axis=-1, keepdims=True))
               - jnp.exp(jnp.sum(lp[2:3, :] * lp[3:4, :], axis=-1, keepdims=True)) + lambda_init)
        o = (acc_ref[0:t, :] / jnp.tile(l_ref[0], (1, 2))
             - lam * (acc_ref[t:2 * t, :] / jnp.tile(l_ref[1], (1, 2))))
        ms = jnp.mean(o * o, axis=-1, keepdims=True)
        o_ref[...] = (o * lax.rsqrt(ms + EPS) * sw_ref[...] * (1.0 - lambda_init)).astype(o_ref.dtype)


def diff_attention(qn, kn, vb, bias_tiles, lam_params, subln_w, lambda_init):
    B, H, S, W = qn.shape
    t = bias_tiles.shape[-1]
    tk = min(ATTN_KEYS_PER_STEP, S)
    stat = pltpu.VMEM((2, t, LANES), jnp.float32)
    accs = pltpu.VMEM((2 * t, W), jnp.float32)
    return pl.pallas_call(
        functools.partial(_attn_kernel, lambda_init=lambda_init),
        grid=(B, H, S // t, S // tk),
        in_specs=[
            pl.BlockSpec((None, None, t, W), lambda b, h, i, j: (b, h, i, 0)),
            pl.BlockSpec((None, None, tk, W), lambda b, h, i, j: (b, h, j, 0)),
            pl.BlockSpec((None, None, tk, W), lambda b, h, i, j: (b, h, j, 0)),
            pl.BlockSpec((None, BIAS_TILES, t, t), lambda b, h, i, j: (h, 0, 0, 0)),
            pl.BlockSpec((4, DB_HEAD), lambda b, h, i, j: (0, 0)),
            pl.BlockSpec((1, W), lambda b, h, i, j: (0, 0)),
        ],
        out_specs=pl.BlockSpec((None, t, W), lambda b, h, i, j: (b, i, h)),
        out_shape=jax.ShapeDtypeStruct((B, S, DB_WIDTH), jnp.bfloat16),
        scratch_shapes=[stat, stat, accs, stat, stat, accs,
                        pltpu.VMEM((2, t, t), jnp.float32), pltpu.VMEM((2, t, t), jnp.float32),
                        pltpu.VMEM((2 * t, t), jnp.bfloat16), pltpu.VMEM((2 * t, t), jnp.bfloat16),
                        pltpu.VMEM((2 * t, LANES), jnp.float32), pltpu.VMEM((2 * t, LANES), jnp.float32)],
        compiler_params=_cparams(4),
        name="diff_attention",
    )(qn, kn, vb, bias_tiles, lam_params, subln_w.reshape(1, W))


def _trunk(x, mod, layers, weights, bias_tiles):
    for l, p in enumerate(layers):
        m = mod[l]
        lambda_init = 0.8 - 0.6 * math.exp(-0.3 * l)
        h = norm_modulate(x, p["norm_attn_w"], m, 0, 1)
        zg = matmul_grouped(h, weights["w_in"], l, jnp.float32, tm=W_IN_TILE[0], tn=W_IN_TILE[1])
        o_fwd = hgrn_pass(zg, p["gates_fwd"], reverse=False)
        out_a = hgrn_pass(zg, p["gates_bwd"], reverse=True, o_fwd=o_fwd, norm_w=p["hgrn_norm_w"])
        qn, kn, vb = qk_norm(zg, p["q_norm_w"], p["k_norm_w"])
        out_b = diff_attention(qn, kn, vb, bias_tiles, p["diff_lambda"], p["diff_subln_w"], lambda_init)
        x = matmul_residual([out_a, out_b], weights["w_out"], l, x, m, 2, tm=W_OUT_TILE[0], tn=W_OUT_TILE[1])
        h = norm_modulate(x, p["norm_ffn_w"], m, 3, 4)
        ug = matmul_grouped(h, weights["w_up"], l, jnp.float32, tm=W_UP_TILE[0], tn=W_UP_TILE[1])
        act = conv_gate(ug, p["conv_w"], p["conv_b"])
        x = matmul_residual([act], weights["w_down"], l, x, m, 5, tm=W_DOWN_TILE[0], tn=W_DOWN_TILE[1])
    return x


def kernel(x_prompt, x_sample, c_prompt, c_sample, w_ada, b_ada, norm_attn_w, w_in, hgrn_lb, hgrn_norm_w,
           q_norm_w, k_norm_w, diff_lambda, diff_subln_w, rel_bias, w_out, norm_ffn_w, w_up, conv_w, conv_b,
           w_down):
    L = w_ada.shape[0]
    D = x_prompt.shape[-1]
    Bp, Bs = c_prompt.shape[0], c_sample.shape[0]
    rows = -(-(Bp + Bs) // SUBLANES) * SUBLANES
    c_all = jnp.zeros((rows, D), jnp.float32).at[:Bp].set(c_prompt).at[Bp:Bp + Bs].set(c_sample)
    mod = adaln_mod(c_all, w_ada, b_ada)
    mod_p = mod[:, :Bp].reshape(L, Bp, 6, 1, D)
    mod_s = mod[:, Bp:Bp + Bs].reshape(L, Bs, 6, 1, D)

    lb_cum = jnp.cumsum(jax.nn.softmax(hgrn_lb.astype(jnp.float32), axis=0), axis=0)
    lb_all = lb_cum - lb_cum[0:1]
    log_lb, log_1m_lb, one_m_lb = jnp.log(lb_all), jnp.log1p(-lb_all), 1.0 - lb_all

    layers = []
    for l in range(L):
        def gates(d):
            return (log_lb[l, d][None], log_1m_lb[l, d][None], one_m_lb[l, d][None])
        layers.append(dict(
            norm_attn_w=norm_attn_w[l], gates_fwd=gates(0), gates_bwd=gates(1), hgrn_norm_w=hgrn_norm_w[l],
            q_norm_w=q_norm_w[l], k_norm_w=k_norm_w[l], diff_lambda=diff_lambda[l].astype(jnp.float32),
            diff_subln_w=diff_subln_w[l], norm_ffn_w=norm_ffn_w[l], conv_w=conv_w[l], conv_b=conv_b[l]))
    weights = dict(w_in=w_in.astype(jnp.bfloat16), w_out=w_out.astype(jnp.bfloat16),
                   w_up=w_up.astype(jnp.bfloat16), w_down=w_down.astype(jnp.bfloat16))
    rel_bias = rel_bias.astype(jnp.float32)
    tiles = {}
    outs = []
    for x, m in ((x_prompt, mod_p), (x_sample, mod_s)):
        S = x.shape[1]
        tile = min(ATTN_TILE, S)
        if tile not in tiles:
            tiles[tile] = rel_bias_tiles(rel_bias, tile)
        outs.append(_trunk(x, m, layers, weights, tiles[tile]))
    return tuple(outs)
```
